```python
import jax, jax.numpy as jnp
from jax import lax
import numpy as np

D_MODEL = 2048
BATCH = 2
SEQ = 16384
DEPTH = 4

PLE_DIM = 256
MIX_WIDTH = D_MODEL
HG_KEY = 128
HG_VAL = 128
HG_WIDTH = MIX_WIDTH // 2
HG_HEADS = HG_WIDTH // HG_KEY
ATT_HEAD_DIM = 128
ATT_WIDTH = MIX_WIDTH - HG_WIDTH
ATT_HEADS = ATT_WIDTH // ATT_HEAD_DIM
DILATED_BRANCHES = ((128, 1), (512, 4), (2048, 16))
ATT_BLOCK = 128
CHUNK = 64
D_FF = 5632
CONV_WIDTH = 3
EPS = 1e-6
IN_SIZES = (HG_WIDTH, HG_WIDTH, HG_WIDTH, HG_WIDTH, ATT_WIDTH, ATT_WIDTH, ATT_WIDTH)
IN_COLS = sum(IN_SIZES)

kernel_name = 'hymba_style_hgrn2_dilated_attn_trunk'


def rms_norm(x, w):
    x32 = x.astype(jnp.float32)
    y = x32 * lax.rsqrt(jnp.mean(x32 * x32, axis=-1, keepdims=True) + EPS)
    return (y * w.astype(jnp.float32)).astype(x.dtype)


def hgrn2_mixer(q_raw, f_raw, i_raw, g_raw, lb, norm_w):
    B, S, _ = q_raw.shape
    f32 = jnp.float32
    q = jax.nn.silu(q_raw.astype(f32)).reshape(B, S, HG_HEADS, HG_KEY)
    lbh = lb.astype(f32).reshape(HG_HEADS, HG_KEY)
    f = lbh + (1.0 - lbh) * jax.nn.sigmoid(f_raw.astype(f32).reshape(B, S, HG_HEADS, HG_KEY))
    k = 1.0 - f
    logf = jnp.log(f)
    v = i_raw.astype(f32).reshape(B, S, HG_HEADS, HG_VAL)
    n_chunks = S // CHUNK

    def to_chunks(t):
        return t.reshape(B, n_chunks, CHUNK, HG_HEADS, t.shape[-1]).transpose(1, 0, 3, 2, 4)

    qc, kc, vc = to_chunks(q), to_chunks(k), to_chunks(v)
    bc = jnp.cumsum(to_chunks(logf), axis=3)
    causal = jnp.tril(jnp.ones((CHUNK, CHUNK), dtype=bool))

    def step(state, xs):
        qt, kt, vt, bt = xs
        inter = jnp.einsum('bhck,bhkv->bhcv', qt * jnp.exp(bt), state)
        rel = bt[:, :, :, None, :] - bt[:, :, None, :, :]
        decay = jnp.exp(jnp.where(causal[:, :, None], rel, -jnp.inf))
        scores = jnp.einsum('bhtk,bhsk,bhtsk->bhts', qt, kt, decay)
        intra = jnp.einsum('bhts,bhsv->bhtv', scores, vt)
        b_last = bt[:, :, -1, :]
        new_state = jnp.exp(b_last)[..., None] * state + jnp.einsum(
            'bhsk,bhsv->bhkv', kt * jnp.exp(b_last[:, :, None, :] - bt), vt)
        return new_state, inter + intra

    s0 = jnp.zeros((B, HG_HEADS, HG_KEY, HG_VAL), f32)
    _, oc = lax.scan(step, s0, (qc, kc, vc, bc))
    o = oc.transpose(1, 0, 3, 2, 4).reshape(B, S, HG_HEADS, HG_VAL)
    o = o * lax.rsqrt(jnp.mean(o * o, axis=-1, keepdims=True) + EPS)
    o = o.reshape(B, S, HG_WIDTH) * norm_w.astype(f32) * jax.nn.silu(g_raw.astype(f32))
    return o.astype(q_raw.dtype)


def dilated_branch(q, k, v, window, dilation):
    B, Sp, H, E = q.shape
    span = window // dilation
    nb = Sp // dilation // ATT_BLOCK

    def to_blocks(t):
        return t.reshape(B, nb, ATT_BLOCK, dilation, H, E).transpose(0, 3, 4, 1, 2, 5)

    def with_prev(t):
        prev = jnp.pad(t[:, :, :, :-1], ((0, 0), (0, 0), (0, 0), (1, 0), (0, 0), (0, 0)))
        return jnp.concatenate([prev, t], axis=4)

    qb = to_blocks(q)
    kw, vw = with_prev(to_blocks(k)), with_prev(to_blocks(v))
    s = jnp.einsum('brhnqe,brhnke->brhnqk', qb, kw) * (E ** -0.5)
    qi = jnp.arange(ATT_BLOCK)[:, None]
    kj = jnp.arange(2 * ATT_BLOCK)[None, :]
    dist = ATT_BLOCK + qi - kj
    blk = jnp.arange(nb)[:, None, None]
    valid = (dist >= 0) & (dist <= span) & ((blk > 0) | (kj >= ATT_BLOCK)[None])
    s = jnp.where(valid, s, -jnp.inf)
    lse = jax.nn.logsumexp(s, axis=-1)
    o = jnp.einsum('brhnqk,brhnke->brhnqe', jnp.exp(s - lse[..., None]), vw)
    o = o.transpose(0, 3, 4, 1, 2, 5).reshape(B, Sp, H, E)
    lse = lse.transpose(0, 3, 4, 1, 2).reshape(B, Sp, H)
    return o, lse


def dilated_attention_mixer(q_raw, k_raw, v_raw):
    B, S, _ = q_raw.shape
    unit = ATT_BLOCK * max(d for _, d in DILATED_BRANCHES)
    Sp = -(-S // unit) * unit

    def prep(t):
        t = t.astype(jnp.float32).reshape(B, S, ATT_HEADS, ATT_HEAD_DIM)
        return jnp.pad(t, ((0, 0), (0, Sp - S), (0, 0), (0, 0)))

    q, k, v = prep(q_raw), prep(k_raw), prep(v_raw)
    branches = [dilated_branch(q, k, v, w, d) for w, d in DILATED_BRANCHES]
    outs = jnp.stack([o for o, _ in branches])
    lses = jnp.stack([l for _, l in branches])
    wts = jax.nn.softmax(lses, axis=0)
    o = jnp.einsum('nbsh,nbshe->bshe', wts, outs)[:, :S]
    return o.reshape(B, S, ATT_WIDTH).astype(q_raw.dtype)


def conv_gated_mlp(u, w_up, conv_w, conv_b, w_down):
    S = u.shape[1]
    up = u @ w_up
    pad = jnp.pad(up, ((0, 0), (CONV_WIDTH - 1, 0), (0, 0)))
    up = conv_b + sum(pad[:, j:j + S] * conv_w[j] for j in range(CONV_WIDTH))
    gate, val = jnp.split(up, 2, axis=-1)
    return (jax.nn.gelu(gate, approximate=True) * val) @ w_down


def setup_inputs(seed: int = 0) -> dict:
    key = jax.random.key(seed)
    ks = jax.random.split(key, 20)
    f32 = jnp.float32

    def nrm(k, shape, scale):
        return jax.random.normal(k, shape, f32) * scale

    def gain(k, shape):
        return 1.0 + 0.05 * jax.random.normal(k, shape, f32)

    L = DEPTH
    return {
        'x': nrm(ks[0], (BATCH, SEQ, D_MODEL), 1.0),
        'p': nrm(ks[1], (DEPTH, BATCH, SEQ, PLE_DIM), 1.0),
        'ln_mix_pre': gain(ks[2], (L, D_MODEL)),
        'w_in': nrm(ks[3], (L, D_MODEL, IN_COLS), D_MODEL ** -0.5),
        'lb_logits': nrm(ks[4], (L, HG_WIDTH), 0.1),
        'hgrn_norm': gain(ks[5], (L, HG_WIDTH)),
        'w_out': nrm(ks[6], (L, MIX_WIDTH, D_MODEL), MIX_WIDTH ** -0.5),
        'ln_mix_post': gain(ks[7], (L, D_MODEL)),
        'ln_ffn_pre': gain(ks[8], (L, D_MODEL)),
        'w_up': nrm(ks[9], (L, D_MODEL, 2 * D_FF), D_MODEL ** -0.5),
        'conv_w': nrm(ks[10], (L, CONV_WIDTH, 2 * D_FF), CONV_WIDTH ** -0.5),
        'conv_b': nrm(ks[11], (L, 2 * D_FF), 0.02),
        'w_down': nrm(ks[12], (L, D_FF, D_MODEL), D_FF ** -0.5),
        'ln_ffn_post': gain(ks[13], (L, D_MODEL)),
        'w_pe': nrm(ks[14], (L, PLE_DIM, D_MODEL), PLE_DIM ** -0.5),
        'w_pg': nrm(ks[15], (L, D_MODEL, D_MODEL), D_MODEL ** -0.5),
    }


def reference(x, p, ln_mix_pre, w_in, lb_logits, hgrn_norm, w_out, ln_mix_post,
              ln_ffn_pre, w_up, conv_w, conv_b, w_down, ln_ffn_post, w_pe, w_pg):
    lb_all = jnp.cumsum(jax.nn.softmax(lb_logits.astype(jnp.float32), axis=0), axis=0)
    lb_all = lb_all - lb_all[0]
    split_at = [int(c) for c in np.cumsum(IN_SIZES)[:-1]]
    h = x
    for l in range(DEPTH):
        u = rms_norm(h, ln_mix_pre[l])
        hq, hf, hi, hg, aq, ak, av = jnp.split(u @ w_in[l], split_at, axis=-1)
        o_rec = hgrn2_mixer(hq, hf, hi, hg, lb_all[l], hgrn_norm[l])
        o_att = dilated_attention_mixer(aq, ak, av)
        mixed = jnp.concatenate([o_rec, o_att], axis=-1) @ w_out[l]
        h = h + rms_norm(mixed, ln_mix_post[l])
        y = conv_gated_mlp(rms_norm(h, ln_ffn_pre[l]), w_up[l], conv_w[l], conv_b[l], w_down[l])
        h = h + rms_norm(y, ln_ffn_post[l])
        h = h + (p[l] @ w_pe[l]) * jax.nn.sigmoid(h @ w_pg[l])
    return h
```

```python
import functools

import numpy as np
import jax
import jax.numpy as jnp
from jax import lax
from jax.experimental import pallas as pl
from jax.experimental.pallas import tpu as pltpu

F32 = jnp.float32
BF16 = jnp.bfloat16

EPS = 1e-6
HEAD_DIM = 128
N_HEADS = 8
GROUP_WIDTH = N_HEADS * HEAD_DIM
N_STREAMS = 7
CHUNK = 64
ATT_BLOCK = 128
DILATIONS = (1, 4, 16)
CONV_WIDTH = 3
HALO = 16
MIB = 1024 * 1024
V7X_VMEM_CAP_MIB = 56


def _params(semantics, vmem_mib):
    return pltpu.CompilerParams(
        dimension_semantics=semantics,
        vmem_limit_bytes=min(vmem_mib, V7X_VMEM_CAP_MIB) * MIB,
    )


def _rms(x, gain):
    return x * lax.rsqrt(jnp.mean(x * x, axis=-1, keepdims=True) + EPS) * gain


def _dot(a, b):
    return jnp.dot(a, b, preferred_element_type=F32)


def _dot_nt(a, b):
    return lax.dot_general(a, b, (((1,), (1,)), ((), ())), preferred_element_type=F32)


def _dot_tn(a, b):
    return lax.dot_general(a, b, (((0,), (0,)), ((), ())), preferred_element_type=F32)


def _lb_kernel(logit_ref, o_ref):
    x = logit_ref[...]
    n_layers = x.shape[0]
    e = jnp.exp(x - jnp.max(x, axis=0, keepdims=True))
    sm = e / jnp.sum(e, axis=0, keepdims=True)
    run = sm[0:1]
    rows = [run]
    for l in range(1, n_layers):
        run = run + sm[l:l + 1]
        rows.append(run)
    for l in range(n_layers):
        o_ref[l:l + 1, :] = rows[l] - rows[0]


def lower_bounds(lb_logits):
    return pl.pallas_call(
        _lb_kernel,
        out_shape=jax.ShapeDtypeStruct(lb_logits.shape, F32),
        name="lower_bounds",
    )(lb_logits.astype(F32))


def _norm_matmul_kernel(x_ref, g_ref, w_ref, o_ref, xn_ref):
    @pl.when(pl.program_id(1) == 0)
    def _():
        xn_ref[...] = _rms(x_ref[...], g_ref[...]).astype(BF16)

    o_ref[...] = _dot(xn_ref[...], w_ref[...]).astype(o_ref.dtype)


def norm_matmul(x, gain, w, layer, *, tm=1024, tn=1024):
    m, d = x.shape
    n = w.shape[-1]
    tm, tn = min(tm, m), min(tn, n)
    vmem = (2 * tm * d * 4 + tm * d * 2 + 2 * d * tn * 2 + 2 * tm * tn * 2 + tm * tn * 4) // MIB + 6
    return pl.pallas_call(
        _norm_matmul_kernel,
        grid=(m // tm, n // tn),
        in_specs=[
            pl.BlockSpec((tm, d), lambda i, j: (i, 0)),
            pl.BlockSpec((None, 1, d), lambda i, j: (layer, 0, 0)),
            pl.BlockSpec((None, d, tn), lambda i, j: (layer, 0, j)),
        ],
        out_specs=pl.BlockSpec((tm, tn), lambda i, j: (i, j)),
        out_shape=jax.ShapeDtypeStruct((m, n), BF16),
        scratch_shapes=[pltpu.VMEM((tm, d), BF16)],
        compiler_params=_params(("parallel", "arbitrary"), vmem),
        name="norm_in_proj",
    )(x, gain, w)


def _hgrn_tables():
    c = CHUNK
    t = np.arange(c)
    cum = (t[None, :] <= t[:, None]).astype(np.float32)
    exps = [cum]
    ups, masks = [], []
    half = c // 2
    while half >= 1:
        blk = 2 * half
        mid = (t // blk) * blk + half
        upper = (t % blk) >= half
        u = t[None, :]
        n_up = upper[:, None] & (u >= mid[:, None]) & (u <= t[:, None])
        n_lo = (~upper)[:, None] & (u >= t[:, None] + 1) & (u <= mid[:, None] - 1)
        exps.append((n_up | n_lo).astype(np.float32))
        ups.append(np.broadcast_to(upper[:, None], (c, HEAD_DIM)).astype(np.float32))
        same = (t[:, None] // blk) == (t[None, :] // blk)
        masks.append((upper[:, None] & (~upper)[None, :] & same).astype(np.float32))
        half //= 2
    masks.append(np.eye(c, dtype=np.float32))
    n_all = np.concatenate(exps, axis=0)
    n3 = np.concatenate([n_all, n_all, n_all], axis=1)
    return n3, np.stack(ups), np.stack(masks)


def _split3(x):
    hi = x.astype(BF16)
    r = x - hi.astype(F32)
    mid = r.astype(BF16)
    lo = (r - mid.astype(F32)).astype(BF16)
    return jnp.concatenate([hi, mid, lo], axis=0)


def _hgrn_kernel(q_ref, f_ref, i_ref, g_ref, lb_ref, nw_ref, n3_ref, up_ref, mask_ref,
                 o_ref, state_ref):
    c = CHUNK
    n_levels = up_ref.shape[0]

    @pl.when(pl.program_id(2) == 0)
    def _():
        state_ref[...] = jnp.zeros_like(state_ref)

    lb = lb_ref[...]
    nw = nw_ref[...]

    def chunk(ci, state):
        rows = pl.ds(pl.multiple_of(ci * c, c), c)
        q = jax.nn.silu(q_ref[rows, :].astype(F32))
        f = lb + (1.0 - lb) * jax.nn.sigmoid(f_ref[rows, :].astype(F32))
        k = 1.0 - f
        v = i_ref[rows, :].astype(BF16)
        e_all = _dot(n3_ref[...], _split3(jnp.log(f)))
        b = e_all[0:c]
        b_last = b[c - 1:c]

        scores = _dot_nt(q.astype(BF16), k.astype(BF16)) * mask_ref[n_levels]
        for l in range(n_levels):
            x = (jnp.where(up_ref[l] > 0.5, q, k) * jnp.exp(e_all[(l + 1) * c:(l + 2) * c])).astype(BF16)
            scores = scores + _dot_nt(x, x) * mask_ref[l]

        o = _dot_nt((q * jnp.exp(b)).astype(BF16), state.astype(BF16))
        o = o + _dot(scores.astype(BF16), v)
        k_dec = (k * jnp.exp(b_last - b)).astype(BF16)
        state = state * jnp.exp(b_last) + _dot_tn(v, k_dec)

        o = o * lax.rsqrt(jnp.mean(o * o, axis=-1, keepdims=True) + EPS)
        o = o * nw * jax.nn.silu(g_ref[rows, :].astype(F32))
        o_ref[rows, :] = o.astype(o_ref.dtype)
        return state

    state_ref[...] = lax.fori_loop(0, q_ref.shape[0] // c, chunk, state_ref[...])


def hgrn2(proj, lb_all, norm_w, layer, *, tt=512):
    bsz, s, _ = proj.shape
    tt = min(tt, s)
    n3, ups, masks = _hgrn_tables()
    stream = lambda j: pl.BlockSpec((None, tt, HEAD_DIM), lambda b, h, t: (b, t, j * N_HEADS + h))
    vec = pl.BlockSpec((None, 1, HEAD_DIM), lambda b, h, t: (layer, 0, h))
    const = lambda a: pl.BlockSpec(a.shape, lambda b, h, t: (0,) * a.ndim)
    return pl.pallas_call(
        _hgrn_kernel,
        grid=(bsz, N_HEADS, s // tt),
        in_specs=[stream(0), stream(1), stream(2), stream(3), vec, vec,
                  const(n3), const(ups), const(masks)],
        out_specs=pl.BlockSpec((None, tt, HEAD_DIM), lambda b, h, t: (b, t, h)),
        out_shape=jax.ShapeDtypeStruct((bsz, s, GROUP_WIDTH), BF16),
        scratch_shapes=[pltpu.VMEM((HEAD_DIM, HEAD_DIM), F32)],
        compiler_params=_params(("parallel", "parallel", "arbitrary"), 16),
        name="hgrn2",
    )(proj, proj, proj, proj, lb_all, norm_w,
      jnp.asarray(n3, BF16), jnp.asarray(ups, F32), jnp.asarray(masks, F32))


def _attn_kernel(*refs, merge, emit_lse):
    q_ref, kp_ref, kc_ref, vp_ref, vc_ref = refs[:5]
    refs = refs[5:]
    if merge:
        oin_ref, lin_ref = refs[:2]
        refs = refs[2:]
    o_ref = refs[0]
    lout_ref = refs[1] if emit_lse else None

    blk = ATT_BLOCK
    n = pl.program_id(2)
    qi = lax.broadcasted_iota(jnp.int32, (blk, 2 * blk), 0)
    kj = lax.broadcasted_iota(jnp.int32, (blk, 2 * blk), 1)
    valid = (kj >= qi) & (kj <= qi + blk) & ((n > 0) | (kj >= blk))
    bias = jnp.where(valid, 0.0, -jnp.inf).astype(F32)
    scale = HEAD_DIM ** -0.5
    lane = lax.broadcasted_iota(jnp.int32, (blk, HEAD_DIM), 1)

    lse_tile = jnp.zeros((blk, HEAD_DIM), F32)
    for h in range(N_HEADS):
        cols = slice(h * HEAD_DIM, (h + 1) * HEAD_DIM)
        kh = jnp.concatenate([kp_ref[:, cols], kc_ref[:, cols]], axis=0)
        vh = jnp.concatenate([vp_ref[:, cols], vc_ref[:, cols]], axis=0)
        s = _dot_nt(q_ref[:, cols], kh) * scale + bias
        m = jnp.max(s, axis=-1, keepdims=True)
        p = jnp.exp(s - m)
        den = jnp.sum(p, axis=-1, keepdims=True)
        o = _dot(p.astype(BF16), vh) / den
        lse = m + jnp.log(den)
        if merge:
            lse_prev = lin_ref[:, h:h + 1]
            top = jnp.maximum(lse, lse_prev)
            w_prev = jnp.exp(lse_prev - top)
            w_new = jnp.exp(lse - top)
            tot = w_prev + w_new
            o = (oin_ref[:, cols] * w_prev + o * w_new) / tot
            lse = top + jnp.log(tot)
        o_ref[:, cols] = o.astype(o_ref.dtype)
        if emit_lse:
            lse_tile = jnp.where(lane == h, lse, lse_tile)
    if emit_lse:
        lout_ref[...] = lse_tile


def dilated_attention(proj):
    bsz, s, width = proj.shape
    blk = ATT_BLOCK
    assert s % (blk * max(DILATIONS)) == 0
    o_acc = lse_acc = None
    for bi, d in enumerate(DILATIONS):
        merge = bi > 0
        last = bi == len(DILATIONS) - 1
        nb = s // d // blk
        view = proj.reshape(bsz, s // d, d * width)
        tile = lambda col, prev: pl.BlockSpec(
            (None, blk, GROUP_WIDTH),
            (lambda b, r, n: (b, jnp.maximum(n - 1, 0), r * N_STREAMS + col)) if prev
            else (lambda b, r, n: (b, n, r * N_STREAMS + col)))
        o_spec = pl.BlockSpec((None, blk, GROUP_WIDTH), lambda b, r, n: (b, n, r))
        l_spec = pl.BlockSpec((None, blk, HEAD_DIM), lambda b, r, n: (b, n, r))
        in_specs = [tile(4, False), tile(5, True), tile(5, False), tile(6, True), tile(6, False)]
        args = [view] * 5
        if merge:
            in_specs += [o_spec, l_spec]
            args += [o_acc.reshape(bsz, s // d, d * GROUP_WIDTH), lse_acc.reshape(bsz, s // d, d * HEAD_DIM)]
        o_shape = jax.ShapeDtypeStruct((bsz, s // d, d * GROUP_WIDTH), BF16 if last else F32)
        l_shape = jax.ShapeDtypeStruct((bsz, s // d, d * HEAD_DIM), F32)
        outs = pl.pallas_call(
            functools.partial(_attn_kernel, merge=merge, emit_lse=not last),
            grid=(bsz, d, nb),
            in_specs=in_specs,
            out_specs=o_spec if last else (o_spec, l_spec),
            out_shape=o_shape if last else (o_shape, l_shape),
            compiler_params=_params(("parallel", "parallel", "arbitrary"), 24),
            name=f"dilated_attn_d{d}",
        )(*args)
        if last:
            o_acc = outs
        else:
            o_acc, lse_acc = outs
        o_acc = o_acc.reshape(bsz, s, GROUP_WIDTH)
        if not last:
            lse_acc = lse_acc.reshape(bsz, s, HEAD_DIM)
    return o_acc


def _out_proj_kernel(h_ref, a1_ref, a2_ref, w_ref, g_ref, o_ref):
    half = a1_ref.shape[1]
    y = _dot(a1_ref[...], w_ref[0:half, :]) + _dot(a2_ref[...], w_ref[half:2 * half, :])
    o_ref[...] = h_ref[...] + _rms(y, g_ref[...])


def out_proj(h, a1, a2, w, gain, layer, *, tm=512):
    m, d = h.shape
    ka = a1.shape[1]
    tm = min(tm, m)
    vmem = (4 * tm * d * 4 + 4 * tm * ka * 2 + 2 * 2 * ka * d * 2 + 2 * tm * d * 4) // MIB + 6
    return pl.pallas_call(
        _out_proj_kernel,
        grid=(m // tm,),
        in_specs=[
            pl.BlockSpec((tm, d), lambda i: (i, 0)),
            pl.BlockSpec((tm, ka), lambda i: (i, 0)),
            pl.BlockSpec((tm, ka), lambda i: (i, 0)),
            pl.BlockSpec((None, 2 * ka, d), lambda i: (layer, 0, 0)),
            pl.BlockSpec((None, 1, d), lambda i: (layer, 0, 0)),
        ],
        out_specs=pl.BlockSpec((tm, d), lambda i: (i, 0)),
        out_shape=jax.ShapeDtypeStruct((m, d), F32),
        compiler_params=_params(("parallel",), vmem),
        name="out_proj_norm_res",
    )(h, a1, a2, w, gain)


def _ffn_kernel(h_ref, halo_ref, g1_ref, wg_ref, wv_ref, cwg_ref, cwv_ref, cbg_ref, cbv_ref,
                wd_ref, g2_ref, o_ref, xn_ref, acc_ref, *, tiles_per_seq):
    i, j = pl.program_id(0), pl.program_id(1)
    tm = h_ref.shape[0]

    @pl.when(j == 0)
    def _():
        first = (i % tiles_per_seq) == 0
        halo = jnp.where(first, 0.0, _rms(halo_ref[...], g1_ref[...]))
        xn_ref[0:HALO, :] = halo.astype(BF16)
        xn_ref[HALO:HALO + tm, :] = _rms(h_ref[...], g1_ref[...]).astype(BF16)
        acc_ref[...] = jnp.zeros_like(acc_ref)

    xn = xn_ref[...]

    def conv(w_ref, cw_ref, cb_ref):
        u = _dot(xn, w_ref[...])
        cw = cw_ref[...]
        y = cb_ref[...] + cw[2:3] * u + cw[1:2] * pltpu.roll(u, 1, 0) + cw[0:1] * pltpu.roll(u, 2, 0)
        return y[HALO:HALO + tm]

    gate = conv(wg_ref, cwg_ref, cbg_ref)
    val = conv(wv_ref, cwv_ref, cbv_ref)
    mid = (jax.nn.gelu(gate, approximate=True) * val).astype(BF16)
    acc_ref[...] += _dot(mid, wd_ref[...])

    @pl.when(j == pl.num_programs(1) - 1)
    def _():
        o_ref[...] = h_ref[...] + _rms(acc_ref[...], g2_ref[...])


def conv_gated_mlp(h, seq_len, g1, w_up, conv_w, conv_b, w_down, g2, layer, *, tm=512, tf=512):
    m, d = h.shape
    f = w_down.shape[1]
    tm, tf = min(tm, seq_len), min(tf, f)
    nf = f // tf
    assert seq_len % tm == 0 and f % tf == 0 and tm % HALO == 0
    halo_blocks = tm // HALO
    vmem = (4 * tm * d * 4 + (tm + HALO) * d * 2 + tm * d * 4 + 3 * 2 * d * tf * 2
            + 3 * (tm + HALO) * tf * 4) // MIB + 8
    gain = pl.BlockSpec((None, 1, d), lambda i, j: (layer, 0, 0))
    return pl.pallas_call(
        functools.partial(_ffn_kernel, tiles_per_seq=seq_len // tm),
        grid=(m // tm, nf),
        in_specs=[
            pl.BlockSpec((tm, d), lambda i, j: (i, 0)),
            pl.BlockSpec((HALO, d), lambda i, j: (jnp.maximum(i * halo_blocks - 1, 0), 0)),
            gain,
            pl.BlockSpec((None, d, tf), lambda i, j: (layer, 0, j)),
            pl.BlockSpec((None, d, tf), lambda i, j: (layer, 0, nf + j)),
            pl.BlockSpec((None, CONV_WIDTH, tf), lambda i, j: (layer, 0, j)),
            pl.BlockSpec((None, CONV_WIDTH, tf), lambda i, j: (layer, 0, nf + j)),
            pl.BlockSpec((None, 1, tf), lambda i, j: (layer, 0, j)),
            pl.BlockSpec((None, 1, tf), lambda i, j: (layer, 0, nf + j)),
            pl.BlockSpec((None, tf, d), lambda i, j: (layer, j, 0)),
            gain,
        ],
        out_specs=pl.BlockSpec((tm, d), lambda i, j: (i, 0)),
        out_shape=jax.ShapeDtypeStruct((m, d), F32),
        scratch_shapes=[pltpu.VMEM((tm + HALO, d), BF16), pltpu.VMEM((tm, d), F32)],
        compiler_params=_params(("parallel", "arbitrary"), vmem),
        name="conv_gated_mlp",
    )(h, h, g1, w_up, w_up, conv_w, conv_w, conv_b, conv_b, w_down, g2)


def _ple_kernel(h_ref, p_ref, wpe_ref, wpg_ref, o_ref):
    h = h_ref[...]
    emb = _dot(p_ref[...].astype(BF16), wpe_ref[...])
    gate = jax.nn.sigmoid(_dot(h.astype(BF16), wpg_ref[...]))
    o_ref[...] = h + emb * gate


def per_layer_embedding(h, p, w_pe, w_pg, layer, *, tm=512):
    m, d = h.shape
    pd = p.shape[-1]
    tm = min(tm, m)
    vmem = (4 * tm * d * 4 + 2 * tm * pd * 4 + 2 * (pd + d) * d * 2 + 3 * tm * d * 4) // MIB + 6
    return pl.pallas_call(
        _ple_kernel,
        grid=(m // tm,),
        in_specs=[
            pl.BlockSpec((tm, d), lambda i: (i, 0)),
            pl.BlockSpec((None, tm, pd), lambda i: (layer, i, 0)),
            pl.BlockSpec((None, pd, d), lambda i: (layer, 0, 0)),
            pl.BlockSpec((None, d, d), lambda i: (layer, 0, 0)),
        ],
        out_specs=pl.BlockSpec((tm, d), lambda i: (i, 0)),
        out_shape=jax.ShapeDtypeStruct((m, d), F32),
        compiler_params=_params(("parallel",), vmem),
        name="per_layer_embedding",
    )(h, p, w_pe, w_pg)


def kernel(x, p, ln_mix_pre, w_in, lb_logits, hgrn_norm, w_out, ln_mix_post, ln_ffn_pre, w_up,
           conv_w, conv_b, w_down, ln_ffn_post, w_pe, w_pg):
    bsz, s, d = x.shape
    depth = w_in.shape[0]
    m = bsz * s
    row = lambda a: a.astype(F32).reshape(depth, 1, a.shape[-1])
    w_in_b, w_out_b, w_up_b, w_down_b = (w.astype(BF16) for w in (w_in, w_out, w_up, w_down))
    w_pe_b, w_pg_b = w_pe.astype(BF16), w_pg.astype(BF16)
    g_mix_pre, g_mix_post, g_ffn_pre, g_ffn_post = map(row, (ln_mix_pre, ln_mix_post, ln_ffn_pre, ln_ffn_post))
    hgrn_w, conv_bias = row(hgrn_norm), row(conv_b)
    lb_all = lower_bounds(lb_logits).reshape(depth, 1, GROUP_WIDTH)
    p2 = p.reshape(depth, m, p.shape[-1])

    h = x.reshape(m, d)
    for l in range(depth):
        proj = norm_matmul(h, g_mix_pre, w_in_b, l).reshape(bsz, s, N_STREAMS * GROUP_WIDTH)
        o_rec = hgrn2(proj, lb_all, hgrn_w, l)
        o_att = dilated_attention(proj)
        h = out_proj(h, o_rec.reshape(m, GROUP_WIDTH), o_att.reshape(m, GROUP_WIDTH), w_out_b, g_mix_post, l)
        h = conv_gated_mlp(h, s, g_ffn_pre, w_up_b, conv_w, conv_bias, w_down_b, g_ffn_post, l)
        h = per_layer_embedding(h, p2, w_pe_b, w_pg_b, l)
    return h.reshape(bsz, s, d)
```

```python
import functools

import numpy as np
import jax
import jax.numpy as jnp
from jax import lax
from jax.experimental import pallas as pl
from jax.experimental.pallas import tpu as pltpu

F32 = jnp.float32
BF16 = jnp.bfloat16

EPS = 1e-6
HEAD_DIM = 128
N_HEADS = 8
GROUP_WIDTH = N_HEADS * HEAD_DIM
N_STREAMS = 7
CHUNK = 64
ATT_BLOCK = 128
DILATIONS = (1, 4, 16)
CONV_WIDTH = 3
HALO = 16
MIB = 1024 * 1024
V7X_VMEM_CAP_MIB = 56


def _params(semantics, vmem_mib):
    return pltpu.CompilerParams(
        dimension_semantics=semantics,
        vmem_limit_bytes=min(vmem_mib, V7X_VMEM_CAP_MIB) * MIB,
    )


def _rms(x, gain):
    return x * lax.rsqrt(jnp.mean(x * x, axis=-1, keepdims=True) + EPS) * gain


def _dot(a, b):
    return jnp.dot(a, b, preferred_element_type=F32)


def _dot_nt(a, b):
    return lax.dot_general(a, b, (((1,), (1,)), ((), ())), preferred_element_type=F32)


def _dot_tn(a, b):
    return lax.dot_general(a, b, (((0,), (0,)), ((), ())), preferred_element_type=F32)


def _lb_kernel(logit_ref, o_ref):
    x = logit_ref[...]
    n_layers = x.shape[0]
    e = jnp.exp(x - jnp.max(x, axis=0, keepdims=True))
    sm = e / jnp.sum(e, axis=0, keepdims=True)
    run = sm[0:1]
    rows = [run]
    for l in range(1, n_layers):
        run = run + sm[l:l + 1]
        rows.append(run)
    for l in range(n_layers):
        o_ref[l:l + 1, :] = rows[l] - rows[0]


def lower_bounds(lb_logits):
    return pl.pallas_call(
        _lb_kernel,
        out_shape=jax.ShapeDtypeStruct(lb_logits.shape, F32),
        name="lower_bounds",
    )(lb_logits.astype(F32))


def _norm_matmul_kernel(x_ref, g_ref, w_ref, o_ref, xn_ref):
    @pl.when(pl.program_id(1) == 0)
    def _():
        xn_ref[...] = _rms(x_ref[...], g_ref[...]).astype(BF16)

    o_ref[...] = _dot(xn_ref[...], w_ref[...]).astype(o_ref.dtype)


def norm_matmul(x, gain, w, layer, n_cols, *, tm=1024, tn=1024):
    m, d = x.shape
    tm = min(tm, m)
    vmem = (2 * tm * d * 4 + tm * d * 2 + 2 * d * tn * 2 + 2 * tm * tn * 2 + tm * tn * 4) // MIB + 6
    return pl.pallas_call(
        _norm_matmul_kernel,
        grid=(m // tm, n_cols // tn),
        in_specs=[
            pl.BlockSpec((tm, d), lambda i, j: (i, 0)),
            pl.BlockSpec((None, 1, d), lambda i, j: (layer, 0, 0)),
            pl.BlockSpec((None, d, tn), lambda i, j: (layer, 0, j)),
        ],
        out_specs=pl.BlockSpec((tm, tn), lambda i, j: (i, j)),
        out_shape=jax.ShapeDtypeStruct((m, n_cols), BF16),
        scratch_shapes=[pltpu.VMEM((tm, d), BF16)],
        compiler_params=_params(("parallel", "arbitrary"), vmem),
        name="norm_in_proj",
    )(x, gain, w)


def _norm_matmul_planes_kernel(x_ref, g_ref, w_ref, o1_ref, o4_ref, o16_ref, xn_ref, nat_ref, pl4_ref):
    @pl.when(pl.program_id(2) == 0)
    def _():
        xn_ref[...] = _rms(x_ref[...], g_ref[...]).astype(BF16)

    res = _dot(xn_ref[...], w_ref[...])
    tm, tn = res.shape
    o1_ref[...] = res.astype(BF16)
    n_slabs = tn // HEAD_DIM
    q4, q16 = tm // 4, tm // 16
    for c in range(n_slabs):
        nat_ref[c] = res[:, c * HEAD_DIM:(c + 1) * HEAD_DIM]
    for c in range(n_slabs):
        cols = slice(c * HEAD_DIM, (c + 1) * HEAD_DIM)
        for r4 in range(4):
            plane = nat_ref[c, pl.ds(r4, q4, stride=4), :]
            o4_ref[r4, :, cols] = plane.astype(BF16)
            pl4_ref[c, r4 * q4:(r4 + 1) * q4, :] = plane
    for c in range(n_slabs):
        cols = slice(c * HEAD_DIM, (c + 1) * HEAD_DIM)
        for r4 in range(4):
            for sub in range(4):
                plane = pl4_ref[c, pl.ds(r4 * q4 + sub, q16, stride=4), :]
                o16_ref[4 * sub + r4, :, cols] = plane.astype(BF16)


def norm_matmul_planes(x, gain, w, layer, col0, n_cols, *, tm=1024, tn=512):
    bsz, s, d = x.shape
    tm = min(tm, s)
    assert DILATIONS == (1, 4, 16)
    assert s % tm == 0 and tm % 256 == 0 and col0 % tn == 0 and n_cols % tn == 0
    n_slabs = tn // HEAD_DIM
    vmem = (2 * tm * d * 4 + tm * d * 2 + 2 * d * tn * 2 + 3 * 2 * tm * tn * 2 + 3 * tm * tn * 4) // MIB + 6
    return pl.pallas_call(
        _norm_matmul_planes_kernel,
        grid=(bsz, s // tm, n_cols // tn),
        in_specs=[
            pl.BlockSpec((None, tm, d), lambda b, i, j: (b, i, 0)),
            pl.BlockSpec((None, 1, d), lambda b, i, j: (layer, 0, 0)),
            pl.BlockSpec((None, d, tn), lambda b, i, j: (layer, 0, col0 // tn + j)),
        ],
        out_specs=(
            pl.BlockSpec((None, None, tm, tn), lambda b, i, j: (b, 0, i, j)),
            pl.BlockSpec((None, 4, tm // 4, tn), lambda b, i, j: (b, 0, i, j)),
            pl.BlockSpec((None, 16, tm // 16, tn), lambda b, i, j: (b, 0, i, j)),
        ),
        out_shape=(
            jax.ShapeDtypeStruct((bsz, 1, s, n_cols), BF16),
            jax.ShapeDtypeStruct((bsz, 4, s // 4, n_cols), BF16),
            jax.ShapeDtypeStruct((bsz, 16, s // 16, n_cols), BF16),
        ),
        scratch_shapes=[pltpu.VMEM((tm, d), BF16),
                        pltpu.VMEM((n_slabs, tm, HEAD_DIM), F32),
                        pltpu.VMEM((n_slabs, tm, HEAD_DIM), F32)],
        compiler_params=_params(("parallel", "parallel", "arbitrary"), vmem),
        name="norm_in_proj_planes",
    )(x, gain, w)


def _hgrn_tables():
    c = CHUNK
    t = np.arange(c)
    cum = (t[None, :] <= t[:, None]).astype(np.float32)
    exps = [cum]
    ups, masks = [], []
    half = c // 2
    while half >= 1:
        blk = 2 * half
        mid = (t // blk) * blk + half
        upper = (t % blk) >= half
        u = t[None, :]
        n_up = upper[:, None] & (u >= mid[:, None]) & (u <= t[:, None])
        n_lo = (~upper)[:, None] & (u >= t[:, None] + 1) & (u <= mid[:, None] - 1)
        exps.append((n_up | n_lo).astype(np.float32))
        ups.append(np.broadcast_to(upper[:, None], (c, HEAD_DIM)).astype(np.float32))
        same = (t[:, None] // blk) == (t[None, :] // blk)
        masks.append((upper[:, None] & (~upper)[None, :] & same).astype(np.float32))
        half //= 2
    masks.append(np.eye(c, dtype=np.float32))
    n_all = np.concatenate(exps, axis=0)
    n3 = np.concatenate([n_all, n_all, n_all], axis=1)
    return n3, np.stack(ups), np.stack(masks)


def _split3(x):
    hi = x.astype(BF16)
    r = x - hi.astype(F32)
    mid = r.astype(BF16)
    lo = (r - mid.astype(F32)).astype(BF16)
    return jnp.concatenate([hi, mid, lo], axis=0)


def _hgrn_kernel(q_ref, f_ref, i_ref, g_ref, lb_ref, nw_ref, n3_ref, up_ref, mask_ref,
                 o_ref, state_ref, e_ref):
    c = CHUNK
    n_levels = up_ref.shape[0]

    @pl.when(pl.program_id(1) == 0)
    def _():
        state_ref[...] = jnp.zeros_like(state_ref)

    def chunk(ci, carry):
        rows = pl.ds(pl.multiple_of(ci * c, c), c)
        lb = lb_ref[...]
        q_all, i_all, g_all = q_ref[rows, :], i_ref[rows, :], g_ref[rows, :]
        f_all = lb + (1.0 - lb) * jax.nn.sigmoid(f_ref[rows, :].astype(F32))
        e_ref[...] = _dot(n3_ref[...], _split3(jnp.log(f_all)))

        outs = []
        for h in range(N_HEADS):
            cols = slice(h * HEAD_DIM, (h + 1) * HEAD_DIM)
            q = jax.nn.silu(q_all[:, cols].astype(F32))
            k = 1.0 - f_all[:, cols]
            v = i_all[:, cols]
            b = e_ref[0:c, cols]
            b_last = b[c - 1:c]
            state = state_ref[h]

            scores = _dot_nt(q.astype(BF16), k.astype(BF16)) * mask_ref[n_levels]
            for l in range(n_levels):
                x = (jnp.where(up_ref[l] > 0.5, q, k) * jnp.exp(e_ref[(l + 1) * c:(l + 2) * c, cols])).astype(BF16)
                scores = scores + _dot_nt(x, x) * mask_ref[l]

            o = _dot_nt((q * jnp.exp(b)).astype(BF16), state.astype(BF16))
            o = o + _dot(scores.astype(BF16), v)
            k_dec = (k * jnp.exp(b_last - b)).astype(BF16)
            state_ref[h] = state * jnp.exp(b_last) + _dot_tn(v, k_dec)

            o = o * lax.rsqrt(jnp.mean(o * o, axis=-1, keepdims=True) + EPS)
            o = o * nw_ref[:, cols] * jax.nn.silu(g_all[:, cols].astype(F32))
            outs.append(o.astype(o_ref.dtype))
        o_ref[rows, :] = jnp.concatenate(outs, axis=-1)
        return carry

    lax.fori_loop(0, q_ref.shape[0] // c, chunk, 0)


def hgrn2(proj, lb_all, norm_w, layer, *, tt=256):
    bsz, s, _ = proj.shape
    tt = min(tt, s)
    n3, ups, masks = _hgrn_tables()
    stream = lambda j: pl.BlockSpec((None, tt, GROUP_WIDTH), lambda b, t: (b, t, j))
    vec = pl.BlockSpec((None, 1, GROUP_WIDTH), lambda b, t: (layer, 0, 0))
    const = lambda a: pl.BlockSpec(a.shape, lambda b, t: (0,) * a.ndim)
    return pl.pallas_call(
        _hgrn_kernel,
        grid=(bsz, s // tt),
        in_specs=[stream(0), stream(1), stream(2), stream(3), vec, vec,
                  const(n3), const(ups), const(masks)],
        out_specs=pl.BlockSpec((None, tt, GROUP_WIDTH), lambda b, t: (b, t, 0)),
        out_shape=jax.ShapeDtypeStruct((bsz, s, GROUP_WIDTH), BF16),
        scratch_shapes=[pltpu.VMEM((N_HEADS, HEAD_DIM, HEAD_DIM), F32),
                        pltpu.VMEM((n3.shape[0], GROUP_WIDTH), F32)],
        compiler_params=_params(("parallel", "arbitrary"), 24),
        name="hgrn2",
    )(proj, proj, proj, proj, lb_all, norm_w,
      jnp.asarray(n3, BF16), jnp.asarray(ups, F32), jnp.asarray(masks, F32))


def _attn_kernel(*refs, merge, emit_lse):
    q_ref, kp_ref, kc_ref, vp_ref, vc_ref = refs[:5]
    refs = refs[5:]
    if merge:
        oin_ref, lin_ref = refs[:2]
        refs = refs[2:]
    o_ref = refs[0]
    refs = refs[1:]
    if emit_lse:
        lout_ref = refs[0]
        refs = refs[1:]
    if merge:
        osc_ref, lsc_ref = refs

    blk = ATT_BLOCK
    rows_per_step = q_ref.shape[0]
    n = pl.program_id(2)

    if merge:
        quarter = rows_per_step // 4
        for sub in range(4):
            for h in range(N_HEADS):
                osc_ref[h, pl.ds(sub, quarter, stride=4), :] = (
                    oin_ref[sub, :, h * HEAD_DIM:(h + 1) * HEAD_DIM].astype(F32))
            lsc_ref[pl.ds(sub, quarter, stride=4), :] = lin_ref[sub]

    qi = lax.broadcasted_iota(jnp.int32, (blk, 2 * blk), 0)
    kj = lax.broadcasted_iota(jnp.int32, (blk, 2 * blk), 1)
    band = (kj >= qi) & (kj <= qi + blk)
    bias_inner = jnp.where(band, 0.0, -jnp.inf).astype(F32)
    bias_first = jnp.where(band & ((n > 0) | (kj >= blk)), 0.0, -jnp.inf).astype(F32)
    scale = HEAD_DIM ** -0.5
    lane = lax.broadcasted_iota(jnp.int32, (blk, HEAD_DIM), 1)

    for j in range(rows_per_step // blk):
        rows = slice(j * blk, (j + 1) * blk)
        bias = bias_first if j == 0 else bias_inner
        lse_tile = jnp.zeros((blk, HEAD_DIM), F32)
        for h in range(N_HEADS):
            cols = slice(h * HEAD_DIM, (h + 1) * HEAD_DIM)
            if j == 0:
                kh = jnp.concatenate([kp_ref[:, cols], kc_ref[0:blk, cols]], axis=0)
                vh = jnp.concatenate([vp_ref[:, cols], vc_ref[0:blk, cols]], axis=0)
            else:
                kh = kc_ref[(j - 1) * blk:(j + 1) * blk, cols]
                vh = vc_ref[(j - 1) * blk:(j + 1) * blk, cols]
            s = _dot_nt(q_ref[rows, cols], kh) * scale + bias
            m = jnp.max(s, axis=-1, keepdims=True)
            p = jnp.exp(s - m)
            den = jnp.sum(p, axis=-1, keepdims=True)
            o = _dot(p.astype(BF16), vh) / den
            lse = m + jnp.log(den)
            if merge:
                lse_prev = lsc_ref[rows, h:h + 1]
                top = jnp.maximum(lse, lse_prev)
                w_prev = jnp.exp(lse_prev - top)
                w_new = jnp.exp(lse - top)
                tot = w_prev + w_new
                o = (osc_ref[h, rows, :] * w_prev + o * w_new) / tot
                lse = top + jnp.log(tot)
            o_ref[rows, cols] = o.astype(o_ref.dtype)
            if emit_lse:
                lse_tile = jnp.where(lane == h, lse, lse_tile)
        if emit_lse:
            lout_ref[rows, :] = lse_tile


def dilated_attention(planes, *, rows_per_step=512):
    blk = ATT_BLOCK
    o_acc = lse_acc = None
    for bi in reversed(range(len(DILATIONS))):
        d = DILATIONS[bi]
        qkv = planes[bi]
        bsz, _, rows, _ = qkv.shape
        step = min(rows_per_step, rows)
        assert rows % step == 0 and step % blk == 0
        merge = o_acc is not None
        last = bi == 0
        cur = lambda col: pl.BlockSpec((None, None, step, GROUP_WIDTH), lambda b, r, n: (b, r, n, col))
        prev = lambda col: pl.BlockSpec(
            (None, None, blk, GROUP_WIDTH),
            lambda b, r, n: (b, r, jnp.maximum(n * (step // blk) - 1, 0), col))
        in_specs = [cur(0), prev(1), cur(1), prev(2), cur(2)]
        args = [qkv] * 5
        scratch = []
        if merge:
            assert DILATIONS[bi + 1] == 4 * d
            coarse = lambda width: pl.BlockSpec((None, 4, None, step // 4, width),
                                                lambda b, r, n: (b, 0, r, n, 0))
            in_specs += [coarse(GROUP_WIDTH), coarse(HEAD_DIM)]
            args += [o_acc.reshape(bsz, 4, d, rows // 4, GROUP_WIDTH),
                     lse_acc.reshape(bsz, 4, d, rows // 4, HEAD_DIM)]
            scratch = [pltpu.VMEM((N_HEADS, step, HEAD_DIM), F32), pltpu.VMEM((step, HEAD_DIM), F32)]
        o_spec = pl.BlockSpec((None, None, step, GROUP_WIDTH), lambda b, r, n: (b, r, n, 0))
        l_spec = pl.BlockSpec((None, None, step, HEAD_DIM), lambda b, r, n: (b, r, n, 0))
        o_shape = jax.ShapeDtypeStruct((bsz, d, rows, GROUP_WIDTH), BF16)
        l_shape = jax.ShapeDtypeStruct((bsz, d, rows, HEAD_DIM), F32)
        outs = pl.pallas_call(
            functools.partial(_attn_kernel, merge=merge, emit_lse=not last),
            grid=(bsz, d, rows // step),
            in_specs=in_specs,
            out_specs=o_spec if last else (o_spec, l_spec),
            out_shape=o_shape if last else (o_shape, l_shape),
            scratch_shapes=scratch,
            compiler_params=_params(("parallel", "parallel", "arbitrary"), 32),
            name=f"dilated_attn_d{d}",
        )(*args)
        o_acc, lse_acc = (outs, None) if last else outs
    bsz, _, s, _ = o_acc.shape
    return o_acc.reshape(bsz, s, GROUP_WIDTH)


def _out_proj_kernel(h_ref, a1_ref, a2_ref, w_ref, g_ref, o_ref):
    half = a1_ref.shape[1]
    y = _dot(a1_ref[...], w_ref[0:half, :]) + _dot(a2_ref[...], w_ref[half:2 * half, :])
    o_ref[...] = h_ref[...] + _rms(y, g_ref[...])


def out_proj(h, a1, a2, w, gain, layer, *, tm=512):
    m, d = h.shape
    ka = a1.shape[1]
    tm = min(tm, m)
    vmem = (4 * tm * d * 4 + 4 * tm * ka * 2 + 2 * 2 * ka * d * 2 + 2 * tm * d * 4) // MIB + 6
    return pl.pallas_call(
        _out_proj_kernel,
        grid=(m // tm,),
        in_specs=[
            pl.BlockSpec((tm, d), lambda i: (i, 0)),
            pl.BlockSpec((tm, ka), lambda i: (i, 0)),
            pl.BlockSpec((tm, ka), lambda i: (i, 0)),
            pl.BlockSpec((None, 2 * ka, d), lambda i: (layer, 0, 0)),
            pl.BlockSpec((None, 1, d), lambda i: (layer, 0, 0)),
        ],
        out_specs=pl.BlockSpec((tm, d), lambda i: (i, 0)),
        out_shape=jax.ShapeDtypeStruct((m, d), F32),
        compiler_params=_params(("parallel",), vmem),
        name="out_proj_norm_res",
    )(h, a1, a2, w, gain)


def _ffn_kernel(h_ref, halo_ref, g1_ref, wg_ref, wv_ref, cwg_ref, cwv_ref, cbg_ref, cbv_ref,
                wd_ref, g2_ref, o_ref, xn_ref, acc_ref, *, tiles_per_seq):
    i, j = pl.program_id(0), pl.program_id(1)
    tm = h_ref.shape[0]

    @pl.when(j == 0)
    def _():
        first = (i % tiles_per_seq) == 0
        halo = jnp.where(first, 0.0, _rms(halo_ref[...], g1_ref[...]))
        xn_ref[0:HALO, :] = halo.astype(BF16)
        xn_ref[HALO:HALO + tm, :] = _rms(h_ref[...], g1_ref[...]).astype(BF16)
        acc_ref[...] = jnp.zeros_like(acc_ref)

    xn = xn_ref[...]

    def conv(w_ref, cw_ref, cb_ref):
        u = _dot(xn, w_ref[...])
        cw = cw_ref[...]
        y = cb_ref[...] + cw[2:3] * u + cw[1:2] * pltpu.roll(u, 1, 0) + cw[0:1] * pltpu.roll(u, 2, 0)
        return y[HALO:HALO + tm]

    gate = conv(wg_ref, cwg_ref, cbg_ref)
    val = conv(wv_ref, cwv_ref, cbv_ref)
    mid = (jax.nn.gelu(gate, approximate=True) * val).astype(BF16)
    acc_ref[...] += _dot(mid, wd_ref[...])

    @pl.when(j == pl.num_programs(1) - 1)
    def _():
        o_ref[...] = h_ref[...] + _rms(acc_ref[...], g2_ref[...])


def conv_gated_mlp(h, seq_len, g1, w_up, conv_w, conv_b, w_down, g2, layer, *, tm=512, tf=512):
    m, d = h.shape
    f = w_down.shape[1]
    tm, tf = min(tm, seq_len), min(tf, f)
    nf = f // tf
    assert seq_len % tm == 0 and f % tf == 0 and tm % HALO == 0
    halo_blocks = tm // HALO
    vmem = (4 * tm * d * 4 + (tm + HALO) * d * 2 + tm * d * 4 + 3 * 2 * d * tf * 2
            + 3 * (tm + HALO) * tf * 4) // MIB + 8
    gain = pl.BlockSpec((None, 1, d), lambda i, j: (layer, 0, 0))
    return pl.pallas_call(
        functools.partial(_ffn_kernel, tiles_per_seq=seq_len // tm),
        grid=(m // tm, nf),
        in_specs=[
            pl.BlockSpec((tm, d), lambda i, j: (i, 0)),
            pl.BlockSpec((HALO, d), lambda i, j: (jnp.maximum(i * halo_blocks - 1, 0), 0)),
            gain,
            pl.BlockSpec((None, d, tf), lambda i, j: (layer, 0, j)),
            pl.BlockSpec((None, d, tf), lambda i, j: (layer, 0, nf + j)),
            pl.BlockSpec((None, CONV_WIDTH, tf), lambda i, j: (layer, 0, j)),
            pl.BlockSpec((None, CONV_WIDTH, tf), lambda i, j: (layer, 0, nf + j)),
            pl.BlockSpec((None, 1, tf), lambda i, j: (layer, 0, j)),
            pl.BlockSpec((None, 1, tf), lambda i, j: (layer, 0, nf + j)),
            pl.BlockSpec((None, tf, d), lambda i, j: (layer, j, 0)),
            gain,
        ],
        out_specs=pl.BlockSpec((tm, d), lambda i, j: (i, 0)),
        out_shape=jax.ShapeDtypeStruct((m, d), F32),
        scratch_shapes=[pltpu.VMEM((tm + HALO, d), BF16), pltpu.VMEM((tm, d), F32)],
        compiler_params=_params(("parallel", "arbitrary"), vmem),
        name="conv_gated_mlp",
    )(h, h, g1, w_up, w_up, conv_w, conv_w, conv_b, conv_b, w_down, g2)


def _ple_kernel(h_ref, p_ref, wpe_ref, wpg_ref, o_ref):
    h = h_ref[...]
    emb = _dot(p_ref[...].astype(BF16), wpe_ref[...])
    gate = jax.nn.sigmoid(_dot(h.astype(BF16), wpg_ref[...]))
    o_ref[...] = h + emb * gate


def per_layer_embedding(h, p, w_pe, w_pg, layer, *, tm=512):
    m, d = h.shape
    pd = p.shape[-1]
    tm = min(tm, m)
    vmem = (4 * tm * d * 4 + 2 * tm * pd * 4 + 2 * (pd + d) * d * 2 + 3 * tm * d * 4) // MIB + 6
    return pl.pallas_call(
        _ple_kernel,
        grid=(m // tm,),
        in_specs=[
            pl.BlockSpec((tm, d), lambda i: (i, 0)),
            pl.BlockSpec((None, tm, pd), lambda i: (layer, i, 0)),
            pl.BlockSpec((None, pd, d), lambda i: (layer, 0, 0)),
            pl.BlockSpec((None, d, d), lambda i: (layer, 0, 0)),
        ],
        out_specs=pl.BlockSpec((tm, d), lambda i: (i, 0)),
        out_shape=jax.ShapeDtypeStruct((m, d), F32),
        compiler_params=_params(("parallel",), vmem),
        name="per_layer_embedding",
    )(h, p, w_pe, w_pg)


def kernel(x, p, ln_mix_pre, w_in, lb_logits, hgrn_norm, w_out, ln_mix_post, ln_ffn_pre, w_up,
           conv_w, conv_b, w_down, ln_ffn_post, w_pe, w_pg):
    bsz, s, d = x.shape
    depth = w_in.shape[0]
    m = bsz * s
    row = lambda a: a.astype(F32).reshape(depth, 1, a.shape[-1])
    w_in_b, w_out_b, w_up_b, w_down_b = (w.astype(BF16) for w in (w_in, w_out, w_up, w_down))
    w_pe_b, w_pg_b = w_pe.astype(BF16), w_pg.astype(BF16)
    g_mix_pre, g_mix_post, g_ffn_pre, g_ffn_post = map(row, (ln_mix_pre, ln_mix_post, ln_ffn_pre, ln_ffn_post))
    hgrn_w, conv_bias = row(hgrn_norm), row(conv_b)
    lb_all = lower_bounds(lb_logits).reshape(depth, 1, GROUP_WIDTH)
    p2 = p.reshape(depth, m, p.shape[-1])

    h = x.reshape(m, d)
    for l in range(depth):
        n_rec = 4 * GROUP_WIDTH
        rec = norm_matmul(h, g_mix_pre, w_in_b, l, n_rec).reshape(bsz, s, n_rec)
        planes = norm_matmul_planes(h.reshape(bsz, s, d), g_mix_pre, w_in_b, l, n_rec, 3 * GROUP_WIDTH)
        o_rec = hgrn2(rec, lb_all, hgrn_w, l)
        o_att = dilated_attention(planes)
        h = out_proj(h, o_rec.reshape(m, GROUP_WIDTH), o_att.reshape(m, GROUP_WIDTH), w_out_b, g_mix_post, l)
        h = conv_gated_mlp(h, s, g_ffn_pre, w_up_b, conv_w, conv_bias, w_down_b, g_ffn_post, l)
        h = per_layer_embedding(h, p2, w_pe_b, w_pg_b, l)
    return h.reshape(bsz, s, d)
```

```python
import functools

import numpy as np
import jax
import jax.numpy as jnp
from jax import lax
from jax.experimental import pallas as pl
from jax.experimental.pallas import tpu as pltpu

F32 = jnp.float32
BF16 = jnp.bfloat16

EPS = 1e-6
LOG2_E = 1.4426950408889634
HEAD_DIM = 128
N_HEADS = 8
GROUP_WIDTH = N_HEADS * HEAD_DIM
N_STREAMS = 7
CHUNK = 64
ATT_BLOCK = 128
DILATIONS = (1, 4, 16)
CONV_WIDTH = 3
HALO = 16
MIB = 1024 * 1024
V7X_VMEM_CAP_MIB = 56


def _params(semantics, vmem_mib):
    return pltpu.CompilerParams(
        dimension_semantics=semantics,
        vmem_limit_bytes=min(vmem_mib, V7X_VMEM_CAP_MIB) * MIB,
    )


def _rms(x, gain):
    return x * lax.rsqrt(jnp.mean(x * x, axis=-1, keepdims=True) + EPS) * gain


def _dot(a, b):
    return jnp.dot(a, b, preferred_element_type=F32)


def _dot_nt(a, b):
    return lax.dot_general(a, b, (((1,), (1,)), ((), ())), preferred_element_type=F32)


def _dot_tn(a, b):
    return lax.dot_general(a, b, (((0,), (0,)), ((), ())), preferred_element_type=F32)


def _lb_kernel(logit_ref, o_ref):
    x = logit_ref[...]
    n_layers = x.shape[0]
    e = jnp.exp(x - jnp.max(x, axis=0, keepdims=True))
    sm = e / jnp.sum(e, axis=0, keepdims=True)
    run = sm[0:1]
    rows = [run]
    for l in range(1, n_layers):
        run = run + sm[l:l + 1]
        rows.append(run)
    for l in range(n_layers):
        o_ref[l:l + 1, :] = rows[l] - rows[0]


def lower_bounds(lb_logits):
    return pl.pallas_call(
        _lb_kernel,
        out_shape=jax.ShapeDtypeStruct(lb_logits.shape, F32),
        name="lower_bounds",
    )(lb_logits.astype(F32))


def _norm_matmul_kernel(x_ref, g_ref, w_ref, o_ref, xn_ref):
    @pl.when(pl.program_id(1) == 0)
    def _():
        xn_ref[...] = _rms(x_ref[...], g_ref[...]).astype(BF16)

    o_ref[...] = _dot(xn_ref[...], w_ref[...]).astype(o_ref.dtype)


def norm_matmul(x, gain, w, layer, n_cols, *, tm=1024, tn=1024):
    m, d = x.shape
    tm = min(tm, m)
    vmem = (2 * tm * d * 4 + tm * d * 2 + 2 * d * tn * 2 + 2 * tm * tn * 2 + tm * tn * 4) // MIB + 6
    return pl.pallas_call(
        _norm_matmul_kernel,
        grid=(m // tm, n_cols // tn),
        in_specs=[
            pl.BlockSpec((tm, d), lambda i, j: (i, 0)),
            pl.BlockSpec((None, 1, d), lambda i, j: (layer, 0, 0)),
            pl.BlockSpec((None, d, tn), lambda i, j: (layer, 0, j)),
        ],
        out_specs=pl.BlockSpec((tm, tn), lambda i, j: (i, j)),
        out_shape=jax.ShapeDtypeStruct((m, n_cols), BF16),
        scratch_shapes=[pltpu.VMEM((tm, d), BF16)],
        compiler_params=_params(("parallel", "arbitrary"), vmem),
        name="norm_in_proj",
    )(x, gain, w)


def _norm_matmul_planes_kernel(x_ref, g_ref, w_ref, o1_ref, o4_ref, o16_ref, xn_ref, nat_ref, pl4_ref):
    @pl.when(pl.program_id(2) == 0)
    def _():
        xn_ref[...] = _rms(x_ref[...], g_ref[...]).astype(BF16)

    res = _dot(xn_ref[...], w_ref[...])
    tm, tn = res.shape
    o1_ref[...] = res.astype(BF16)
    n_slabs = tn // HEAD_DIM
    q4, q16 = tm // 4, tm // 16
    for c in range(n_slabs):
        nat_ref[c] = res[:, c * HEAD_DIM:(c + 1) * HEAD_DIM]
    for c in range(n_slabs):
        cols = slice(c * HEAD_DIM, (c + 1) * HEAD_DIM)
        for r4 in range(4):
            plane = nat_ref[c, pl.ds(r4, q4, stride=4), :]
            o4_ref[r4, :, cols] = plane.astype(BF16)
            pl4_ref[c, r4 * q4:(r4 + 1) * q4, :] = plane
    for c in range(n_slabs):
        cols = slice(c * HEAD_DIM, (c + 1) * HEAD_DIM)
        for r4 in range(4):
            for sub in range(4):
                plane = pl4_ref[c, pl.ds(r4 * q4 + sub, q16, stride=4), :]
                o16_ref[4 * sub + r4, :, cols] = plane.astype(BF16)


def norm_matmul_planes(x, gain, w, layer, col0, n_cols, *, tm=1024, tn=512):
    bsz, s, d = x.shape
    tm = min(tm, s)
    assert DILATIONS == (1, 4, 16)
    assert s % tm == 0 and tm % 256 == 0 and col0 % tn == 0 and n_cols % tn == 0
    n_slabs = tn // HEAD_DIM
    vmem = (2 * tm * d * 4 + tm * d * 2 + 2 * d * tn * 2 + 3 * 2 * tm * tn * 2 + 3 * tm * tn * 4) // MIB + 6
    return pl.pallas_call(
        _norm_matmul_planes_kernel,
        grid=(bsz, s // tm, n_cols // tn),
        in_specs=[
            pl.BlockSpec((None, tm, d), lambda b, i, j: (b, i, 0)),
            pl.BlockSpec((None, 1, d), lambda b, i, j: (layer, 0, 0)),
            pl.BlockSpec((None, d, tn), lambda b, i, j: (layer, 0, col0 // tn + j)),
        ],
        out_specs=(
            pl.BlockSpec((None, None, tm, tn), lambda b, i, j: (b, 0, i, j)),
            pl.BlockSpec((None, 4, tm // 4, tn), lambda b, i, j: (b, 0, i, j)),
            pl.BlockSpec((None, 16, tm // 16, tn), lambda b, i, j: (b, 0, i, j)),
        ),
        out_shape=(
            jax.ShapeDtypeStruct((bsz, 1, s, n_cols), BF16),
            jax.ShapeDtypeStruct((bsz, 4, s // 4, n_cols), BF16),
            jax.ShapeDtypeStruct((bsz, 16, s // 16, n_cols), BF16),
        ),
        scratch_shapes=[pltpu.VMEM((tm, d), BF16),
                        pltpu.VMEM((n_slabs, tm, HEAD_DIM), F32),
                        pltpu.VMEM((n_slabs, tm, HEAD_DIM), F32)],
        compiler_params=_params(("parallel", "parallel", "arbitrary"), vmem),
        name="norm_in_proj_planes",
    )(x, gain, w)


def _hgrn_tables():
    c = CHUNK
    t = np.arange(c)
    cum = (t[None, :] <= t[:, None]).astype(np.float32)
    exps = [cum]
    ups, masks = [], []
    half = c // 2
    while half >= 1:
        blk = 2 * half
        mid = (t // blk) * blk + half
        upper = (t % blk) >= half
        u = t[None, :]
        n_up = upper[:, None] & (u >= mid[:, None]) & (u <= t[:, None])
        n_lo = (~upper)[:, None] & (u >= t[:, None] + 1) & (u <= mid[:, None] - 1)
        exps.append((n_up | n_lo).astype(np.float32))
        ups.append(np.broadcast_to(upper[:, None], (c, HEAD_DIM)).astype(np.float32))
        same = (t[:, None] // blk) == (t[None, :] // blk)
        masks.append((upper[:, None] & (~upper)[None, :] & same).astype(np.float32))
        half //= 2
    masks.append(np.eye(c, dtype=np.float32))
    n_all = np.concatenate(exps, axis=0)
    n3 = np.concatenate([n_all, n_all, n_all], axis=1)
    return n3, np.stack(ups), np.stack(masks)


def _split3(x):
    hi = x.astype(BF16)
    r = x - hi.astype(F32)
    mid = r.astype(BF16)
    lo = (r - mid.astype(F32)).astype(BF16)
    return jnp.concatenate([hi, mid, lo], axis=0)


def _hgrn_kernel(q_ref, f_ref, i_ref, g_ref, lb_ref, nw_ref, n3_ref, up_ref, mask_ref,
                 o_ref, state_ref, e_ref):
    c = CHUNK
    n_levels = up_ref.shape[0]

    @pl.when(pl.program_id(1) == 0)
    def _():
        state_ref[...] = jnp.zeros_like(state_ref)

    def chunk(ci, carry):
        rows = pl.ds(pl.multiple_of(ci * c, c), c)
        lb = lb_ref[...]
        q_all, i_all, g_all = q_ref[rows, :], i_ref[rows, :], g_ref[rows, :]
        f_all = lb + (1.0 - lb) * jax.nn.sigmoid(f_ref[rows, :].astype(F32))
        e_ref[...] = _dot(n3_ref[...], _split3(jnp.log(f_all) * LOG2_E))

        heads = [slice(h * HEAD_DIM, (h + 1) * HEAD_DIM) for h in range(N_HEADS)]
        q = [jax.nn.silu(q_all[:, cols].astype(F32)) for cols in heads]
        k = [1.0 - f_all[:, cols] for cols in heads]
        scores = [_dot_nt(q[h].astype(BF16), k[h].astype(BF16)) * mask_ref[n_levels] for h in range(N_HEADS)]
        for l in range(n_levels):
            half = c >> (l + 1)
            for h, cols in enumerate(heads):
                if half % 8 == 0:
                    qk = jnp.concatenate(
                        [(q[h] if (r // half) % 2 else k[h])[r:r + half] for r in range(0, c, half)], axis=0)
                else:
                    qk = jnp.where(up_ref[l] > 0.5, q[h], k[h])
                x = (qk * jnp.exp2(e_ref[(l + 1) * c:(l + 2) * c, cols])).astype(BF16)
                scores[h] = scores[h] + _dot_nt(x, x) * mask_ref[l]

        outs = []
        for h, cols in enumerate(heads):
            v = i_all[:, cols]
            b = e_ref[0:c, cols]
            b_last = b[c - 1:c]
            state = state_ref[h]
            o = _dot_nt((q[h] * jnp.exp2(b)).astype(BF16), state.astype(BF16))
            o = o + _dot(scores[h].astype(BF16), v)
            k_dec = (k[h] * jnp.exp2(b_last - b)).astype(BF16)
            state_ref[h] = state * jnp.exp2(b_last) + _dot_tn(v, k_dec)
            outs.append(o)
        for h, cols in enumerate(heads):
            o = outs[h]
            o = o * lax.rsqrt(jnp.mean(o * o, axis=-1, keepdims=True) + EPS)
            o = o * nw_ref[:, cols] * jax.nn.silu(g_all[:, cols].astype(F32))
            outs[h] = o.astype(o_ref.dtype)
        o_ref[rows, :] = jnp.concatenate(outs, axis=-1)
        return carry

    lax.fori_loop(0, q_ref.shape[0] // c, chunk, 0)


def hgrn2(proj, lb_all, norm_w, layer, *, tt=256):
    bsz, s, _ = proj.shape
    tt = min(tt, s)
    n3, ups, masks = _hgrn_tables()
    stream = lambda j: pl.BlockSpec((None, tt, GROUP_WIDTH), lambda b, t: (b, t, j))
    vec = pl.BlockSpec((None, 1, GROUP_WIDTH), lambda b, t: (layer, 0, 0))
    const = lambda a: pl.BlockSpec(a.shape, lambda b, t: (0,) * a.ndim)
    return pl.pallas_call(
        _hgrn_kernel,
        grid=(bsz, s // tt),
        in_specs=[stream(0), stream(1), stream(2), stream(3), vec, vec,
                  const(n3), const(ups), const(masks)],
        out_specs=pl.BlockSpec((None, tt, GROUP_WIDTH), lambda b, t: (b, t, 0)),
        out_shape=jax.ShapeDtypeStruct((bsz, s, GROUP_WIDTH), BF16),
        scratch_shapes=[pltpu.VMEM((N_HEADS, HEAD_DIM, HEAD_DIM), F32),
                        pltpu.VMEM((n3.shape[0], GROUP_WIDTH), F32)],
        compiler_params=_params(("parallel", "arbitrary"), 24),
        name="hgrn2",
    )(proj, proj, proj, proj, lb_all, norm_w,
      jnp.asarray(n3, BF16), jnp.asarray(ups, F32), jnp.asarray(masks, F32))


def _attn_kernel(*refs, merge, emit_lse):
    q_ref, kp_ref, kc_ref, vp_ref, vc_ref = refs[:5]
    refs = refs[5:]
    if merge:
        oin_ref, lin_ref = refs[:2]
        refs = refs[2:]
    o_ref = refs[0]
    refs = refs[1:]
    if emit_lse:
        lout_ref = refs[0]
        refs = refs[1:]
    if merge:
        osc_ref, lsc_ref = refs

    blk = ATT_BLOCK
    rows_per_step = q_ref.shape[0]
    n = pl.program_id(2)

    if merge:
        quarter = rows_per_step // 4
        for sub in range(4):
            for h in range(N_HEADS):
                osc_ref[h, pl.ds(sub, quarter, stride=4), :] = (
                    oin_ref[sub, :, h * HEAD_DIM:(h + 1) * HEAD_DIM].astype(F32))
            lsc_ref[pl.ds(sub, quarter, stride=4), :] = lin_ref[sub]

    qi = lax.broadcasted_iota(jnp.int32, (blk, 2 * blk), 0)
    kj = lax.broadcasted_iota(jnp.int32, (blk, 2 * blk), 1)
    band = (kj >= qi) & (kj <= qi + blk)
    bias_inner = jnp.where(band, 0.0, -jnp.inf).astype(F32)
    bias_first = jnp.where(band & ((n > 0) | (kj >= blk)), 0.0, -jnp.inf).astype(F32)
    scale = HEAD_DIM ** -0.5
    lane = lax.broadcasted_iota(jnp.int32, (blk, HEAD_DIM), 1)

    for j in range(rows_per_step // blk):
        rows = slice(j * blk, (j + 1) * blk)
        bias = bias_first if j == 0 else bias_inner
        lse_tile = jnp.zeros((blk, HEAD_DIM), F32)
        for h in range(N_HEADS):
            cols = slice(h * HEAD_DIM, (h + 1) * HEAD_DIM)
            if j == 0:
                kh = jnp.concatenate([kp_ref[:, cols], kc_ref[0:blk, cols]], axis=0)
                vh = jnp.concatenate([vp_ref[:, cols], vc_ref[0:blk, cols]], axis=0)
            else:
                kh = kc_ref[(j - 1) * blk:(j + 1) * blk, cols]
                vh = vc_ref[(j - 1) * blk:(j + 1) * blk, cols]
            s = _dot_nt(q_ref[rows, cols], kh) * scale + bias
            m = jnp.max(s, axis=-1, keepdims=True)
            p = jnp.exp(s - m)
            den = jnp.sum(p, axis=-1, keepdims=True)
            o = _dot(p.astype(BF16), vh) / den
            lse = jnp.broadcast_to(m + jnp.log(den), (blk, HEAD_DIM))
            if merge:
                lse_prev = jnp.broadcast_to(lsc_ref[rows, h:h + 1], (blk, HEAD_DIM))
                top = jnp.maximum(lse, lse_prev)
                w_prev = jnp.exp(lse_prev - top)
                w_new = jnp.exp(lse - top)
                tot = w_prev + w_new
                o = (osc_ref[h, rows, :] * w_prev + o * w_new) / tot
                lse = top + jnp.log(tot)
            o_ref[rows, cols] = o.astype(o_ref.dtype)
            if emit_lse:
                lse_tile = jnp.where(lane == h, lse, lse_tile)
        if emit_lse:
            lout_ref[rows, :] = lse_tile


def dilated_attention(planes, *, rows_per_step=512):
    blk = ATT_BLOCK
    o_acc = lse_acc = None
    for bi in reversed(range(len(DILATIONS))):
        d = DILATIONS[bi]
        qkv = planes[bi]
        bsz, _, rows, _ = qkv.shape
        step = min(rows_per_step, rows)
        assert rows % step == 0 and step % blk == 0
        merge = o_acc is not None
        last = bi == 0
        cur = lambda col: pl.BlockSpec((None, None, step, GROUP_WIDTH), lambda b, r, n: (b, r, n, col))
        prev = lambda col: pl.BlockSpec(
            (None, None, blk, GROUP_WIDTH),
            lambda b, r, n: (b, r, jnp.maximum(n * (step // blk) - 1, 0), col))
        in_specs = [cur(0), prev(1), cur(1), prev(2), cur(2)]
        args = [qkv] * 5
        scratch = []
        if merge:
            assert DILATIONS[bi + 1] == 4 * d
            coarse = lambda width: pl.BlockSpec((None, 4, None, step // 4, width),
                                                lambda b, r, n: (b, 0, r, n, 0))
            in_specs += [coarse(GROUP_WIDTH), coarse(HEAD_DIM)]
            args += [o_acc.reshape(bsz, 4, d, rows // 4, GROUP_WIDTH),
                     lse_acc.reshape(bsz, 4, d, rows // 4, HEAD_DIM)]
            scratch = [pltpu.VMEM((N_HEADS, step, HEAD_DIM), F32), pltpu.VMEM((step, HEAD_DIM), F32)]
        o_spec = pl.BlockSpec((None, None, step, GROUP_WIDTH), lambda b, r, n: (b, r, n, 0))
        l_spec = pl.BlockSpec((None, None, step, HEAD_DIM), lambda b, r, n: (b, r, n, 0))
        o_shape = jax.ShapeDtypeStruct((bsz, d, rows, GROUP_WIDTH), BF16)
        l_shape = jax.ShapeDtypeStruct((bsz, d, rows, HEAD_DIM), F32)
        outs = pl.pallas_call(
            functools.partial(_attn_kernel, merge=merge, emit_lse=not last),
            grid=(bsz, d, rows // step),
            in_specs=in_specs,
            out_specs=o_spec if last else (o_spec, l_spec),
            out_shape=o_shape if last else (o_shape, l_shape),
            scratch_shapes=scratch,
            compiler_params=_params(("parallel", "parallel", "arbitrary"), 32),
            name=f"dilated_attn_d{d}",
        )(*args)
        o_acc, lse_acc = (outs, None) if last else outs
    bsz, _, s, _ = o_acc.shape
    return o_acc.reshape(bsz, s, GROUP_WIDTH)


def _out_proj_kernel(h_ref, a1_ref, a2_ref, w_ref, g_ref, o_ref):
    half = a1_ref.shape[1]
    y = _dot(a1_ref[...], w_ref[0:half, :]) + _dot(a2_ref[...], w_ref[half:2 * half, :])
    o_ref[...] = h_ref[...] + _rms(y, g_ref[...])


def out_proj(h, a1, a2, w, gain, layer, *, tm=512):
    m, d = h.shape
    ka = a1.shape[1]
    tm = min(tm, m)
    vmem = (4 * tm * d * 4 + 4 * tm * ka * 2 + 2 * 2 * ka * d * 2 + 2 * tm * d * 4) // MIB + 6
    return pl.pallas_call(
        _out_proj_kernel,
        grid=(m // tm,),
        in_specs=[
            pl.BlockSpec((tm, d), lambda i: (i, 0)),
            pl.BlockSpec((tm, ka), lambda i: (i, 0)),
            pl.BlockSpec((tm, ka), lambda i: (i, 0)),
            pl.BlockSpec((None, 2 * ka, d), lambda i: (layer, 0, 0)),
            pl.BlockSpec((None, 1, d), lambda i: (layer, 0, 0)),
        ],
        out_specs=pl.BlockSpec((tm, d), lambda i: (i, 0)),
        out_shape=jax.ShapeDtypeStruct((m, d), F32),
        compiler_params=_params(("parallel",), vmem),
        name="out_proj_norm_res",
    )(h, a1, a2, w, gain)


def _ffn_kernel(h_ref, halo_ref, g1_ref, wg_ref, wv_ref, cwg_ref, cwv_ref, cbg_ref, cbv_ref,
                wd_ref, g2_ref, o_ref, xn_ref, acc_ref, *, tiles_per_seq, n_sub):
    i, j = pl.program_id(0), pl.program_id(1)
    tm = h_ref.shape[0]

    @pl.when(j == 0)
    def _():
        first = (i % tiles_per_seq) == 0
        halo = jnp.where(first, 0.0, _rms(halo_ref[...], g1_ref[...]))
        xn_ref[0:HALO, :] = halo.astype(BF16)
        xn_ref[HALO:HALO + tm, :] = _rms(h_ref[...], g1_ref[...]).astype(BF16)
        acc_ref[...] = jnp.zeros_like(acc_ref)

    sub = tm // n_sub
    blocks = [slice(0, HALO + sub)] + [slice(HALO + s * sub, HALO + (s + 1) * sub) for s in range(1, n_sub)]

    def conv(u, cw_ref, cb_ref):
        cw = cw_ref[...]
        y = cb_ref[...] + cw[2:3] * u + cw[1:2] * pltpu.roll(u, 1, 0) + cw[0:1] * pltpu.roll(u, 2, 0)
        return y[HALO:HALO + sub]

    ups = [[_dot(xn_ref[rows, :], w_ref[...]) for w_ref in (wg_ref, wv_ref)] for rows in blocks]
    for s in range(n_sub):
        if s > 0:
            ups[s] = [jnp.concatenate([above[-HALO:], u], axis=0) for above, u in zip(ups[s - 1], ups[s])]
        gate = conv(ups[s][0], cwg_ref, cbg_ref)
        val = conv(ups[s][1], cwv_ref, cbv_ref)
        mid = (jax.nn.gelu(gate, approximate=True) * val).astype(BF16)
        acc_ref[s * sub:(s + 1) * sub, :] += _dot(mid, wd_ref[...])

    @pl.when(j == pl.num_programs(1) - 1)
    def _():
        o_ref[...] = h_ref[...] + _rms(acc_ref[...], g2_ref[...])


def conv_gated_mlp(h, seq_len, g1, w_up, conv_w, conv_b, w_down, g2, layer, *, tm=512, tf=512, n_sub=4):
    m, d = h.shape
    f = w_down.shape[1]
    tm, tf = min(tm, seq_len), min(tf, f)
    nf = f // tf
    assert seq_len % tm == 0 and f % tf == 0 and tm % (n_sub * HALO) == 0
    halo_blocks = tm // HALO
    vmem = (4 * tm * d * 4 + (tm + HALO) * d * 2 + tm * d * 4 + 3 * 2 * d * tf * 2
            + 3 * (tm + HALO) * tf * 4) // MIB + 8
    gain = pl.BlockSpec((None, 1, d), lambda i, j: (layer, 0, 0))
    return pl.pallas_call(
        functools.partial(_ffn_kernel, tiles_per_seq=seq_len // tm, n_sub=n_sub),
        grid=(m // tm, nf),
        in_specs=[
            pl.BlockSpec((tm, d), lambda i, j: (i, 0)),
            pl.BlockSpec((HALO, d), lambda i, j: (jnp.maximum(i * halo_blocks - 1, 0), 0)),
            gain,
            pl.BlockSpec((None, d, tf), lambda i, j: (layer, 0, j)),
            pl.BlockSpec((None, d, tf), lambda i, j: (layer, 0, nf + j)),
            pl.BlockSpec((None, CONV_WIDTH, tf), lambda i, j: (layer, 0, j)),
            pl.BlockSpec((None, CONV_WIDTH, tf), lambda i, j: (layer, 0, nf + j)),
            pl.BlockSpec((None, 1, tf), lambda i, j: (layer, 0, j)),
            pl.BlockSpec((None, 1, tf), lambda i, j: (layer, 0, nf + j)),
            pl.BlockSpec((None, tf, d), lambda i, j: (layer, j, 0)),
            gain,
        ],
        out_specs=pl.BlockSpec((tm, d), lambda i, j: (i, 0)),
        out_shape=jax.ShapeDtypeStruct((m, d), F32),
        scratch_shapes=[pltpu.VMEM((tm + HALO, d), BF16), pltpu.VMEM((tm, d), F32)],
        compiler_params=_params(("parallel", "arbitrary"), vmem),
        name="conv_gated_mlp",
    )(h, h, g1, w_up, w_up, conv_w, conv_w, conv_b, conv_b, w_down, g2)


def _ple_kernel(h_ref, p_ref, wpe_ref, wpg_ref, o_ref):
    h = h_ref[...]
    emb = _dot(p_ref[...].astype(BF16), wpe_ref[...])
    gate = jax.nn.sigmoid(_dot(h.astype(BF16), wpg_ref[...]))
    o_ref[...] = h + emb * gate


def per_layer_embedding(h, p, w_pe, w_pg, layer, *, tm=512):
    m, d = h.shape
    pd = p.shape[-1]
    tm = min(tm, m)
    vmem = (4 * tm * d * 4 + 2 * tm * pd * 4 + 2 * (pd + d) * d * 2 + 3 * tm * d * 4) // MIB + 6
    return pl.pallas_call(
        _ple_kernel,
        grid=(m // tm,),
        in_specs=[
            pl.BlockSpec((tm, d), lambda i: (i, 0)),
            pl.BlockSpec((None, tm, pd), lambda i: (layer, i, 0)),
            pl.BlockSpec((None, pd, d), lambda i: (layer, 0, 0)),
            pl.BlockSpec((None, d, d), lambda i: (layer, 0, 0)),
        ],
        out_specs=pl.BlockSpec((tm, d), lambda i: (i, 0)),
        out_shape=jax.ShapeDtypeStruct((m, d), F32),
        compiler_params=_params(("parallel",), vmem),
        name="per_layer_embedding",
    )(h, p, w_pe, w_pg)


def kernel(x, p, ln_mix_pre, w_in, lb_logits, hgrn_norm, w_out, ln_mix_post, ln_ffn_pre, w_up,
           conv_w, conv_b, w_down, ln_ffn_post, w_pe, w_pg):
    bsz, s, d = x.shape
    depth = w_in.shape[0]
    m = bsz * s
    row = lambda a: a.astype(F32).reshape(depth, 1, a.shape[-1])
    w_in_b, w_out_b, w_up_b, w_down_b = (w.astype(BF16) for w in (w_in, w_out, w_up, w_down))
    w_pe_b, w_pg_b = w_pe.astype(BF16), w_pg.astype(BF16)
    g_mix_pre, g_mix_post, g_ffn_pre, g_ffn_post = map(row, (ln_mix_pre, ln_mix_post, ln_ffn_pre, ln_ffn_post))
    hgrn_w, conv_bias = row(hgrn_norm), row(conv_b)
    lb_all = lower_bounds(lb_logits).reshape(depth, 1, GROUP_WIDTH)
    p2 = p.reshape(depth, m, p.shape[-1])

    h = x.reshape(m, d)
    for l in range(depth):
        n_rec = 4 * GROUP_WIDTH
        rec = norm_matmul(h, g_mix_pre, w_in_b, l, n_rec).reshape(bsz, s, n_rec)
        planes = norm_matmul_planes(h.reshape(bsz, s, d), g_mix_pre, w_in_b, l, n_rec, 3 * GROUP_WIDTH)
        o_rec = hgrn2(rec, lb_all, hgrn_w, l)
        o_att = dilated_attention(planes)
        h = out_proj(h, o_rec.reshape(m, GROUP_WIDTH), o_att.reshape(m, GROUP_WIDTH), w_out_b, g_mix_post, l)
        h = conv_gated_mlp(h, s, g_ffn_pre, w_up_b, conv_w, conv_bias, w_down_b, g_ffn_post, l)
        h = per_layer_embedding(h, p2, w_pe_b, w_pg_b, l)
    return h.reshape(bsz, s, d)
```

```python
import functools

import numpy as np
import jax
import jax.numpy as jnp
from jax import lax
from jax.experimental import pallas as pl
from jax.experimental.pallas import tpu as pltpu

F32 = jnp.float32
BF16 = jnp.bfloat16

EPS = 1e-6
LOG2_E = 1.4426950408889634
HEAD_DIM = 128
N_HEADS = 8
GROUP_WIDTH = N_HEADS * HEAD_DIM
N_STREAMS = 7
CHUNK = 64
ATT_BLOCK = 128
DILATIONS = (1, 4, 16)
CONV_WIDTH = 3
HALO = 16
MIB = 1024 * 1024
V7X_VMEM_CAP_MIB = 56


def _params(semantics, vmem_mib):
    return pltpu.CompilerParams(
        dimension_semantics=semantics,
        vmem_limit_bytes=min(vmem_mib, V7X_VMEM_CAP_MIB) * MIB,
    )


def _rms(x, gain):
    return x * lax.rsqrt(jnp.mean(x * x, axis=-1, keepdims=True) + EPS) * gain


def _dot(a, b):
    return jnp.dot(a, b, preferred_element_type=F32)


def _dot_nt(a, b):
    return lax.dot_general(a, b, (((1,), (1,)), ((), ())), preferred_element_type=F32)


def _dot_tn(a, b):
    return lax.dot_general(a, b, (((0,), (0,)), ((), ())), preferred_element_type=F32)


def _lb_kernel(logit_ref, o_ref):
    x = logit_ref[...]
    n_layers = x.shape[0]
    e = jnp.exp(x - jnp.max(x, axis=0, keepdims=True))
    sm = e / jnp.sum(e, axis=0, keepdims=True)
    run = sm[0:1]
    rows = [run]
    for l in range(1, n_layers):
        run = run + sm[l:l + 1]
        rows.append(run)
    for l in range(n_layers):
        o_ref[l:l + 1, :] = rows[l] - rows[0]


def lower_bounds(lb_logits):
    return pl.pallas_call(
        _lb_kernel,
        out_shape=jax.ShapeDtypeStruct(lb_logits.shape, F32),
        name="lower_bounds",
    )(lb_logits.astype(F32))


def _norm_matmul_kernel(x_ref, g_ref, w_ref, o_ref, xn_ref):
    @pl.when(pl.program_id(1) == 0)
    def _():
        xn_ref[...] = _rms(x_ref[...], g_ref[...]).astype(BF16)

    o_ref[...] = _dot(xn_ref[...], w_ref[...]).astype(o_ref.dtype)


def norm_matmul(x, gain, w, layer, n_cols, *, tm=1024, tn=1024):
    m, d = x.shape
    tm = min(tm, m)
    vmem = (2 * tm * d * 4 + tm * d * 2 + 2 * d * tn * 2 + 2 * tm * tn * 2 + tm * tn * 4) // MIB + 6
    return pl.pallas_call(
        _norm_matmul_kernel,
        grid=(m // tm, n_cols // tn),
        in_specs=[
            pl.BlockSpec((tm, d), lambda i, j: (i, 0)),
            pl.BlockSpec((None, 1, d), lambda i, j: (layer, 0, 0)),
            pl.BlockSpec((None, d, tn), lambda i, j: (layer, 0, j)),
        ],
        out_specs=pl.BlockSpec((tm, tn), lambda i, j: (i, j)),
        out_shape=jax.ShapeDtypeStruct((m, n_cols), BF16),
        scratch_shapes=[pltpu.VMEM((tm, d), BF16)],
        compiler_params=_params(("parallel", "arbitrary"), vmem),
        name="norm_in_proj",
    )(x, gain, w)


def _norm_matmul_planes_kernel(x_ref, g_ref, w_ref, o1_ref, o4_ref, o16_ref, xn_ref, nat_ref, pl4_ref):
    @pl.when(pl.program_id(2) == 0)
    def _():
        xn_ref[...] = _rms(x_ref[...], g_ref[...]).astype(BF16)

    res = _dot(xn_ref[...], w_ref[...])
    tm, tn = res.shape
    o1_ref[...] = res.astype(BF16)
    n_slabs = tn // HEAD_DIM
    q4, q16 = tm // 4, tm // 16
    for c in range(n_slabs):
        nat_ref[c] = res[:, c * HEAD_DIM:(c + 1) * HEAD_DIM]
    for c in range(n_slabs):
        cols = slice(c * HEAD_DIM, (c + 1) * HEAD_DIM)
        for r4 in range(4):
            plane = nat_ref[c, pl.ds(r4, q4, stride=4), :]
            o4_ref[r4, :, cols] = plane.astype(BF16)
            pl4_ref[c, r4 * q4:(r4 + 1) * q4, :] = plane
    for c in range(n_slabs):
        cols = slice(c * HEAD_DIM, (c + 1) * HEAD_DIM)
        for r4 in range(4):
            for sub in range(4):
                plane = pl4_ref[c, pl.ds(r4 * q4 + sub, q16, stride=4), :]
                o16_ref[4 * sub + r4, :, cols] = plane.astype(BF16)


def norm_matmul_planes(x, gain, w, layer, col0, n_streams, *, tm=512):
    bsz, s, d = x.shape
    tn = GROUP_WIDTH
    tm = min(tm, s)
    assert DILATIONS == (1, 4, 16)
    assert s % tm == 0 and tm % 256 == 0 and col0 % tn == 0
    n_slabs = tn // HEAD_DIM
    vmem = (2 * tm * d * 4 + tm * d * 2 + 2 * d * tn * 2 + 3 * 2 * tm * tn * 2 + 3 * tm * tn * 4) // MIB + 6
    return pl.pallas_call(
        _norm_matmul_planes_kernel,
        grid=(bsz, s // tm, n_streams),
        in_specs=[
            pl.BlockSpec((None, tm, d), lambda b, i, j: (b, i, 0)),
            pl.BlockSpec((None, 1, d), lambda b, i, j: (layer, 0, 0)),
            pl.BlockSpec((None, d, tn), lambda b, i, j: (layer, 0, col0 // tn + j)),
        ],
        out_specs=(
            pl.BlockSpec((None, None, None, tm, tn), lambda b, i, j: (j, b, 0, i, 0)),
            pl.BlockSpec((None, None, 4, tm // 4, tn), lambda b, i, j: (j, b, 0, i, 0)),
            pl.BlockSpec((None, None, 16, tm // 16, tn), lambda b, i, j: (j, b, 0, i, 0)),
        ),
        out_shape=(
            jax.ShapeDtypeStruct((n_streams, bsz, 1, s, tn), BF16),
            jax.ShapeDtypeStruct((n_streams, bsz, 4, s // 4, tn), BF16),
            jax.ShapeDtypeStruct((n_streams, bsz, 16, s // 16, tn), BF16),
        ),
        scratch_shapes=[pltpu.VMEM((tm, d), BF16),
                        pltpu.VMEM((n_slabs, tm, HEAD_DIM), F32),
                        pltpu.VMEM((n_slabs, tm, HEAD_DIM), F32)],
        compiler_params=_params(("parallel", "parallel", "arbitrary"), vmem),
        name="norm_in_proj_planes",
    )(x, gain, w)


def _hgrn_tables():
    c = CHUNK
    t = np.arange(c)
    cum = (t[None, :] <= t[:, None]).astype(np.float32)
    exps = [cum]
    ups, masks = [], []
    half = c // 2
    while half >= 1:
        blk = 2 * half
        mid = (t // blk) * blk + half
        upper = (t % blk) >= half
        u = t[None, :]
        n_up = upper[:, None] & (u >= mid[:, None]) & (u <= t[:, None])
        n_lo = (~upper)[:, None] & (u >= t[:, None] + 1) & (u <= mid[:, None] - 1)
        exps.append((n_up | n_lo).astype(np.float32))
        ups.append(np.broadcast_to(upper[:, None], (c, HEAD_DIM)).astype(np.float32))
        same = (t[:, None] // blk) == (t[None, :] // blk)
        masks.append((upper[:, None] & (~upper)[None, :] & same).astype(np.float32))
        half //= 2
    masks.append(np.eye(c, dtype=np.float32))
    n_all = np.concatenate(exps, axis=0)
    n3 = np.concatenate([n_all, n_all, n_all], axis=1)
    return n3, np.stack(ups), np.stack(masks)


def _split3(x):
    hi = x.astype(BF16)
    r = x - hi.astype(F32)
    mid = r.astype(BF16)
    lo = (r - mid.astype(F32)).astype(BF16)
    return jnp.concatenate([hi, mid, lo], axis=0)


def _hgrn_kernel(q_ref, f_ref, i_ref, g_ref, lb_ref, nw_ref, n3_ref, up_ref, mask_ref,
                 o_ref, state_ref, e_ref):
    c = CHUNK
    n_levels = up_ref.shape[0]

    @pl.when(pl.program_id(1) == 0)
    def _():
        state_ref[...] = jnp.zeros_like(state_ref)

    n_e = n3_ref.shape[0]
    per_iter = e_ref.shape[0] // n_e
    heads = [slice(h * HEAD_DIM, (h + 1) * HEAD_DIM) for h in range(N_HEADS)]

    def body(it, carry):
        rows = pl.ds(pl.multiple_of(it * (per_iter * c), per_iter * c), per_iter * c)
        lb = lb_ref[...]
        q_all, i_all, g_all = q_ref[rows, :], i_ref[rows, :], g_ref[rows, :]
        f_all = lb + (1.0 - lb) * jax.nn.sigmoid(f_ref[rows, :].astype(F32))
        log2_f = jnp.log(f_all) * LOG2_E
        chains = []
        for u in range(per_iter):
            t = slice(u * c, (u + 1) * c)
            e_ref[u * n_e:(u + 1) * n_e, :] = _dot(n3_ref[...], _split3(log2_f[t]))
            for cols in heads:
                chains.append((u, t, cols))
        q = [jax.nn.silu(q_all[t, cols].astype(F32)) for _, t, cols in chains]
        k = [1.0 - f_all[t, cols] for _, t, cols in chains]
        scores = [_dot_nt(q[n].astype(BF16), k[n].astype(BF16)) * mask_ref[n_levels] for n in range(len(chains))]
        for l in range(n_levels):
            half = c >> (l + 1)
            for n, (u, t, cols) in enumerate(chains):
                if half % 8 == 0:
                    qk = jnp.concatenate(
                        [(q[n] if (r // half) % 2 else k[n])[r:r + half] for r in range(0, c, half)], axis=0)
                else:
                    qk = jnp.where(up_ref[l] > 0.5, q[n], k[n])
                e_rows = slice(u * n_e + (l + 1) * c, u * n_e + (l + 2) * c)
                x = (qk * jnp.exp2(e_ref[e_rows, cols])).astype(BF16)
                scores[n] = scores[n] + _dot_nt(x, x) * mask_ref[l]

        outs = []
        for n, (u, t, cols) in enumerate(chains):
            h = n % N_HEADS
            v = i_all[t, cols]
            b = e_ref[u * n_e:u * n_e + c, cols]
            b_last = b[c - 1:c]
            state = state_ref[h]
            o = _dot_nt((q[n] * jnp.exp2(b)).astype(BF16), state.astype(BF16))
            o = o + _dot(scores[n].astype(BF16), v)
            k_dec = (k[n] * jnp.exp2(b_last - b)).astype(BF16)
            state_ref[h] = state * jnp.exp2(b_last) + _dot_tn(v, k_dec)
            outs.append(o)
        for n, (u, t, cols) in enumerate(chains):
            o = outs[n]
            o = o * lax.rsqrt(jnp.mean(o * o, axis=-1, keepdims=True) + EPS)
            o = o * nw_ref[:, cols] * jax.nn.silu(g_all[t, cols].astype(F32))
            outs[n] = o.astype(o_ref.dtype)
        o_ref[rows, :] = jnp.concatenate(
            [jnp.concatenate(outs[u * N_HEADS:(u + 1) * N_HEADS], axis=-1) for u in range(per_iter)], axis=0)
        return carry

    lax.fori_loop(0, q_ref.shape[0] // (per_iter * c), body, 0)


def hgrn2(proj, lb_all, norm_w, layer, *, tt=256, chunks_per_iter=2):
    bsz, s, _ = proj.shape
    tt = min(tt, s)
    assert s % tt == 0 and tt % (chunks_per_iter * CHUNK) == 0
    n3, ups, masks = _hgrn_tables()
    stream = lambda j: pl.BlockSpec((None, tt, GROUP_WIDTH), lambda b, t: (b, t, j))
    vec = pl.BlockSpec((None, 1, GROUP_WIDTH), lambda b, t: (layer, 0, 0))
    const = lambda a: pl.BlockSpec(a.shape, lambda b, t: (0,) * a.ndim)
    return pl.pallas_call(
        _hgrn_kernel,
        grid=(bsz, s // tt),
        in_specs=[stream(0), stream(1), stream(2), stream(3), vec, vec,
                  const(n3), const(ups), const(masks)],
        out_specs=pl.BlockSpec((None, tt, GROUP_WIDTH), lambda b, t: (b, t, 0)),
        out_shape=jax.ShapeDtypeStruct((bsz, s, GROUP_WIDTH), BF16),
        scratch_shapes=[pltpu.VMEM((N_HEADS, HEAD_DIM, HEAD_DIM), F32),
                        pltpu.VMEM((chunks_per_iter * n3.shape[0], GROUP_WIDTH), F32)],
        compiler_params=_params(("parallel", "arbitrary"), 24),
        name="hgrn2",
    )(proj, proj, proj, proj, lb_all, norm_w,
      jnp.asarray(n3, BF16), jnp.asarray(ups, F32), jnp.asarray(masks, F32))


def _attn_kernel(*refs, merge, emit_lse):
    q_ref, kp_ref, kc_ref, vp_ref, vc_ref = refs[:5]
    refs = refs[5:]
    if merge:
        oin_ref, lin_ref = refs[:2]
        refs = refs[2:]
    o_ref = refs[0]
    refs = refs[1:]
    if emit_lse:
        lout_ref = refs[0]
        refs = refs[1:]
    if merge:
        osc_ref, lsc_ref = refs

    blk = ATT_BLOCK
    rows_per_step = q_ref.shape[0]
    n = pl.program_id(2)

    if merge:
        quarter = rows_per_step // 4
        for sub in range(4):
            for h in range(N_HEADS):
                osc_ref[h, pl.ds(sub, quarter, stride=4), :] = (
                    oin_ref[sub, :, h * HEAD_DIM:(h + 1) * HEAD_DIM].astype(F32))
            lsc_ref[pl.ds(sub, quarter, stride=4), :] = lin_ref[sub]

    qi = lax.broadcasted_iota(jnp.int32, (blk, 2 * blk), 0)
    kj = lax.broadcasted_iota(jnp.int32, (blk, 2 * blk), 1)
    band = (kj >= qi) & (kj <= qi + blk)
    bias_inner = jnp.where(band, 0.0, -jnp.inf).astype(F32)
    bias_first = jnp.where(band & ((n > 0) | (kj >= blk)), 0.0, -jnp.inf).astype(F32)
    scale = HEAD_DIM ** -0.5
    lane = lax.broadcasted_iota(jnp.int32, (blk, HEAD_DIM), 1)

    for j in range(rows_per_step // blk):
        rows = slice(j * blk, (j + 1) * blk)
        bias = bias_first if j == 0 else bias_inner
        lse_tile = jnp.zeros((blk, HEAD_DIM), F32)
        for h in range(N_HEADS):
            cols = slice(h * HEAD_DIM, (h + 1) * HEAD_DIM)
            if j == 0:
                kh = jnp.concatenate([kp_ref[:, cols], kc_ref[0:blk, cols]], axis=0)
                vh = jnp.concatenate([vp_ref[:, cols], vc_ref[0:blk, cols]], axis=0)
            else:
                kh = kc_ref[(j - 1) * blk:(j + 1) * blk, cols]
                vh = vc_ref[(j - 1) * blk:(j + 1) * blk, cols]
            s = _dot_nt(q_ref[rows, cols], kh) * scale + bias
            m = jnp.max(s, axis=-1, keepdims=True)
            p = jnp.exp(s - m)
            den = jnp.sum(p, axis=-1, keepdims=True)
            o = _dot(p.astype(BF16), vh) / den
            lse = jnp.broadcast_to(m + jnp.log(den), (blk, HEAD_DIM))
            if merge:
                lse_prev = jnp.broadcast_to(lsc_ref[rows, h:h + 1], (blk, HEAD_DIM))
                top = jnp.maximum(lse, lse_prev)
                w_prev = jnp.exp(lse_prev - top)
                w_new = jnp.exp(lse - top)
                tot = w_prev + w_new
                o = (osc_ref[h, rows, :] * w_prev + o * w_new) / tot
                lse = top + jnp.log(tot)
            o_ref[rows, cols] = o.astype(o_ref.dtype)
            if emit_lse:
                lse_tile = jnp.where(lane == h, lse, lse_tile)
        if emit_lse:
            lout_ref[rows, :] = lse_tile


def dilated_attention(planes, *, rows_per_step=512):
    blk = ATT_BLOCK
    o_acc = lse_acc = None
    for bi in reversed(range(len(DILATIONS))):
        d = DILATIONS[bi]
        qkv = planes[bi]
        _, bsz, _, rows, _ = qkv.shape
        step = min(rows_per_step, rows)
        assert rows % step == 0 and step % blk == 0
        merge = o_acc is not None
        last = bi == 0
        cur = lambda col: pl.BlockSpec((None, None, None, step, GROUP_WIDTH),
                                       lambda b, r, n: (col, b, r, n, 0))
        prev = lambda col: pl.BlockSpec(
            (None, None, None, blk, GROUP_WIDTH),
            lambda b, r, n: (col, b, r, jnp.maximum(n * (step // blk) - 1, 0), 0))
        in_specs = [cur(0), prev(1), cur(1), prev(2), cur(2)]
        args = [qkv] * 5
        scratch = []
        if merge:
            assert DILATIONS[bi + 1] == 4 * d
            coarse = lambda width: pl.BlockSpec((None, 4, None, step // 4, width),
                                                lambda b, r, n: (b, 0, r, n, 0))
            in_specs += [coarse(GROUP_WIDTH), coarse(HEAD_DIM)]
            args += [o_acc.reshape(bsz, 4, d, rows // 4, GROUP_WIDTH),
                     lse_acc.reshape(bsz, 4, d, rows // 4, HEAD_DIM)]
            scratch = [pltpu.VMEM((N_HEADS, step, HEAD_DIM), F32), pltpu.VMEM((step, HEAD_DIM), F32)]
        o_spec = pl.BlockSpec((None, None, step, GROUP_WIDTH), lambda b, r, n: (b, r, n, 0))
        l_spec = pl.BlockSpec((None, None, step, HEAD_DIM), lambda b, r, n: (b, r, n, 0))
        o_shape = jax.ShapeDtypeStruct((bsz, d, rows, GROUP_WIDTH), BF16)
        l_shape = jax.ShapeDtypeStruct((bsz, d, rows, HEAD_DIM), F32)
        outs = pl.pallas_call(
            functools.partial(_attn_kernel, merge=merge, emit_lse=not last),
            grid=(bsz, d, rows // step),
            in_specs=in_specs,
            out_specs=o_spec if last else (o_spec, l_spec),
            out_shape=o_shape if last else (o_shape, l_shape),
            scratch_shapes=scratch,
            compiler_params=_params(("parallel", "parallel", "arbitrary"), 32),
            name=f"dilated_attn_d{d}",
        )(*args)
        o_acc, lse_acc = (outs, None) if last else outs
    bsz, _, s, _ = o_acc.shape
    return o_acc.reshape(bsz, s, GROUP_WIDTH)


def _out_proj_kernel(h_ref, a1_ref, a2_ref, w_ref, g_ref, o_ref):
    half = a1_ref.shape[1]
    y = _dot(a1_ref[...], w_ref[0:half, :]) + _dot(a2_ref[...], w_ref[half:2 * half, :])
    o_ref[...] = h_ref[...] + _rms(y, g_ref[...])


def out_proj(h, a1, a2, w, gain, layer, *, tm=512):
    m, d = h.shape
    ka = a1.shape[1]
    tm = min(tm, m)
    vmem = (4 * tm * d * 4 + 4 * tm * ka * 2 + 2 * 2 * ka * d * 2 + 2 * tm * d * 4) // MIB + 6
    return pl.pallas_call(
        _out_proj_kernel,
        grid=(m // tm,),
        in_specs=[
            pl.BlockSpec((tm, d), lambda i: (i, 0)),
            pl.BlockSpec((tm, ka), lambda i: (i, 0)),
            pl.BlockSpec((tm, ka), lambda i: (i, 0)),
            pl.BlockSpec((None, 2 * ka, d), lambda i: (layer, 0, 0)),
            pl.BlockSpec((None, 1, d), lambda i: (layer, 0, 0)),
        ],
        out_specs=pl.BlockSpec((tm, d), lambda i: (i, 0)),
        out_shape=jax.ShapeDtypeStruct((m, d), F32),
        compiler_params=_params(("parallel",), vmem),
        name="out_proj_norm_res",
    )(h, a1, a2, w, gain)


def _ffn_kernel(h_ref, halo_ref, g1_ref, wg_ref, wv_ref, cwg_ref, cwv_ref, cbg_ref, cbv_ref,
                wd_ref, g2_ref, o_ref, xn_ref, acc_ref, *, tiles_per_seq):
    i, j = pl.program_id(0), pl.program_id(1)
    tm = h_ref.shape[0]

    @pl.when(j == 0)
    def _():
        first = (i % tiles_per_seq) == 0
        halo = jnp.where(first, 0.0, _rms(halo_ref[...], g1_ref[...]))
        xn_ref[0:HALO, :] = halo.astype(BF16)
        xn_ref[HALO:HALO + tm, :] = _rms(h_ref[...], g1_ref[...]).astype(BF16)
        acc_ref[...] = jnp.zeros_like(acc_ref)

    xn = xn_ref[...]

    def conv(w_ref, cw_ref, cb_ref):
        u = _dot(xn, w_ref[...])
        cw = cw_ref[...]
        y = cb_ref[...] + cw[2:3] * u + cw[1:2] * pltpu.roll(u, 1, 0) + cw[0:1] * pltpu.roll(u, 2, 0)
        return y[HALO:HALO + tm]

    gate = conv(wg_ref, cwg_ref, cbg_ref)
    val = conv(wv_ref, cwv_ref, cbv_ref)
    mid = (jax.nn.gelu(gate, approximate=True) * val).astype(BF16)
    acc_ref[...] += _dot(mid, wd_ref[...])

    @pl.when(j == pl.num_programs(1) - 1)
    def _():
        o_ref[...] = h_ref[...] + _rms(acc_ref[...], g2_ref[...])


def conv_gated_mlp(h, seq_len, g1, w_up, conv_w, conv_b, w_down, g2, layer, *, tm=512, tf=512):
    m, d = h.shape
    f = w_down.shape[1]
    tm, tf = min(tm, seq_len), min(tf, f)
    nf = f // tf
    assert seq_len % tm == 0 and f % tf == 0 and tm % HALO == 0
    halo_blocks = tm // HALO
    vmem = (4 * tm * d * 4 + (tm + HALO) * d * 2 + tm * d * 4 + 3 * 2 * d * tf * 2
            + 3 * (tm + HALO) * tf * 4) // MIB + 8
    gain = pl.BlockSpec((None, 1, d), lambda i, j: (layer, 0, 0))
    return pl.pallas_call(
        functools.partial(_ffn_kernel, tiles_per_seq=seq_len // tm),
        grid=(m // tm, nf),
        in_specs=[
            pl.BlockSpec((tm, d), lambda i, j: (i, 0)),
            pl.BlockSpec((HALO, d), lambda i, j: (jnp.maximum(i * halo_blocks - 1, 0), 0)),
            gain,
            pl.BlockSpec((None, d, tf), lambda i, j: (layer, 0, j)),
            pl.BlockSpec((None, d, tf), lambda i, j: (layer, 0, nf + j)),
            pl.BlockSpec((None, CONV_WIDTH, tf), lambda i, j: (layer, 0, j)),
            pl.BlockSpec((None, CONV_WIDTH, tf), lambda i, j: (layer, 0, nf + j)),
            pl.BlockSpec((None, 1, tf), lambda i, j: (layer, 0, j)),
            pl.BlockSpec((None, 1, tf), lambda i, j: (layer, 0, nf + j)),
            pl.BlockSpec((None, tf, d), lambda i, j: (layer, j, 0)),
            gain,
        ],
        out_specs=pl.BlockSpec((tm, d), lambda i, j: (i, 0)),
        out_shape=jax.ShapeDtypeStruct((m, d), F32),
        scratch_shapes=[pltpu.VMEM((tm + HALO, d), BF16), pltpu.VMEM((tm, d), F32)],
        compiler_params=_params(("parallel", "arbitrary"), vmem),
        name="conv_gated_mlp",
    )(h, h, g1, w_up, w_up, conv_w, conv_w, conv_b, conv_b, w_down, g2)


def _ple_kernel(h_ref, p_ref, wpe_ref, wpg_ref, o_ref):
    h = h_ref[...]
    emb = _dot(p_ref[...].astype(BF16), wpe_ref[...])
    gate = jax.nn.sigmoid(_dot(h.astype(BF16), wpg_ref[...]))
    o_ref[...] = h + emb * gate


def per_layer_embedding(h, p, w_pe, w_pg, layer, *, tm=512):
    m, d = h.shape
    pd = p.shape[-1]
    tm = min(tm, m)
    vmem = (4 * tm * d * 4 + 2 * tm * pd * 4 + 2 * (pd + d) * d * 2 + 3 * tm * d * 4) // MIB + 6
    return pl.pallas_call(
        _ple_kernel,
        grid=(m // tm,),
        in_specs=[
            pl.BlockSpec((tm, d), lambda i: (i, 0)),
            pl.BlockSpec((None, tm, pd), lambda i: (layer, i, 0)),
            pl.BlockSpec((None, pd, d), lambda i: (layer, 0, 0)),
            pl.BlockSpec((None, d, d), lambda i: (layer, 0, 0)),
        ],
        out_specs=pl.BlockSpec((tm, d), lambda i: (i, 0)),
        out_shape=jax.ShapeDtypeStruct((m, d), F32),
        compiler_params=_params(("parallel",), vmem),
        name="per_layer_embedding",
    )(h, p, w_pe, w_pg)


def kernel(x, p, ln_mix_pre, w_in, lb_logits, hgrn_norm, w_out, ln_mix_post, ln_ffn_pre, w_up,
           conv_w, conv_b, w_down, ln_ffn_post, w_pe, w_pg):
    bsz, s, d = x.shape
    depth = w_in.shape[0]
    m = bsz * s
    row = lambda a: a.astype(F32).reshape(depth, 1, a.shape[-1])
    w_in_b, w_out_b, w_up_b, w_down_b = (w.astype(BF16) for w in (w_in, w_out, w_up, w_down))
    w_pe_b, w_pg_b = w_pe.astype(BF16), w_pg.astype(BF16)
    g_mix_pre, g_mix_post, g_ffn_pre, g_ffn_post = map(row, (ln_mix_pre, ln_mix_post, ln_ffn_pre, ln_ffn_post))
    hgrn_w, conv_bias = row(hgrn_norm), row(conv_b)
    lb_all = lower_bounds(lb_logits).reshape(depth, 1, GROUP_WIDTH)
    p2 = p.reshape(depth, m, p.shape[-1])

    h = x.reshape(m, d)
    for l in range(depth):
        n_rec = 4 * GROUP_WIDTH
        rec = norm_matmul(h, g_mix_pre, w_in_b, l, n_rec).reshape(bsz, s, n_rec)
        planes = norm_matmul_planes(h.reshape(bsz, s, d), g_mix_pre, w_in_b, l, n_rec, 3)
        o_rec = hgrn2(rec, lb_all, hgrn_w, l)
        o_att = dilated_attention(planes)
        h = out_proj(h, o_rec.reshape(m, GROUP_WIDTH), o_att.reshape(m, GROUP_WIDTH), w_out_b, g_mix_post, l)
        h = conv_gated_mlp(h, s, g_ffn_pre, w_up_b, conv_w, conv_bias, w_down_b, g_ffn_post, l)
        h = per_layer_embedding(h, p2, w_pe_b, w_pg_b, l)
    return h.reshape(bsz, s, d)
```

```python
import functools

import numpy as np
import jax
import jax.numpy as jnp
from jax import lax
from jax.experimental import pallas as pl
from jax.experimental.pallas import tpu as pltpu

F32 = jnp.float32
BF16 = jnp.bfloat16

EPS = 1e-6
LOG2_E = 1.4426950408889634
HEAD_DIM = 128
N_HEADS = 8
GROUP_WIDTH = N_HEADS * HEAD_DIM
N_STREAMS = 7
CHUNK = 64
ATT_BLOCK = 128
DILATIONS = (1, 4, 16)
CONV_WIDTH = 3
HALO = 16
MIB = 1024 * 1024
V7X_VMEM_CAP_MIB = 56


def _params(semantics, vmem_mib):
    return pltpu.CompilerParams(
        dimension_semantics=semantics,
        vmem_limit_bytes=min(vmem_mib, V7X_VMEM_CAP_MIB) * MIB,
    )


def _rms(x, gain):
    return x * lax.rsqrt(jnp.mean(x * x, axis=-1, keepdims=True) + EPS) * gain


def _dot(a, b):
    return jnp.dot(a, b, preferred_element_type=F32)


def _dot_nt(a, b):
    return lax.dot_general(a, b, (((1,), (1,)), ((), ())), preferred_element_type=F32)


def _dot_tn(a, b):
    return lax.dot_general(a, b, (((0,), (0,)), ((), ())), preferred_element_type=F32)


def _lb_kernel(logit_ref, o_ref):
    x = logit_ref[...]
    n_layers = x.shape[0]
    e = jnp.exp(x - jnp.max(x, axis=0, keepdims=True))
    sm = e / jnp.sum(e, axis=0, keepdims=True)
    run = sm[0:1]
    rows = [run]
    for l in range(1, n_layers):
        run = run + sm[l:l + 1]
        rows.append(run)
    for l in range(n_layers):
        o_ref[l:l + 1, :] = rows[l] - rows[0]


def lower_bounds(lb_logits):
    return pl.pallas_call(
        _lb_kernel,
        out_shape=jax.ShapeDtypeStruct(lb_logits.shape, F32),
        name="lower_bounds",
    )(lb_logits.astype(F32))


def _in_proj_kernel(x_ref, g_ref, w_ref, rec_ref, o1_ref, o4_ref, o16_ref, xn_ref, nat_ref, pl4_ref,
                    *, n_rec):
    j = pl.program_id(2)

    @pl.when(j == 0)
    def _():
        xn_ref[...] = _rms(x_ref[...], g_ref[...]).astype(BF16)

    res = _dot(xn_ref[...], w_ref[...])
    tm, tn = res.shape

    @pl.when(j < n_rec)
    def _():
        rec_ref[...] = res.astype(BF16)

    @pl.when(j >= n_rec)
    def _():
        o1_ref[...] = res.astype(BF16)
        n_slabs = tn // HEAD_DIM
        q4, q16 = tm // 4, tm // 16
        for c in range(n_slabs):
            nat_ref[c] = res[:, c * HEAD_DIM:(c + 1) * HEAD_DIM]
        for c in range(n_slabs):
            cols = slice(c * HEAD_DIM, (c + 1) * HEAD_DIM)
            for r4 in range(4):
                plane = nat_ref[c, pl.ds(r4, q4, stride=4), :]
                o4_ref[r4, :, cols] = plane.astype(BF16)
                pl4_ref[c, r4 * q4:(r4 + 1) * q4, :] = plane
        for c in range(n_slabs):
            cols = slice(c * HEAD_DIM, (c + 1) * HEAD_DIM)
            for r4 in range(4):
                for sub in range(4):
                    plane = pl4_ref[c, pl.ds(r4 * q4 + sub, q16, stride=4), :]
                    o16_ref[4 * sub + r4, :, cols] = plane.astype(BF16)


def in_proj(x, gain, w, layer, n_rec, n_att, *, tm=512):
    bsz, s, d = x.shape
    tn = GROUP_WIDTH
    tm = min(tm, s)
    assert DILATIONS == (1, 4, 16)
    assert s % tm == 0 and tm % 256 == 0
    n_slabs = tn // HEAD_DIM
    vmem = (2 * tm * d * 4 + tm * d * 2 + 2 * d * tn * 2 + 4 * 2 * tm * tn * 2 + 3 * tm * tn * 4) // MIB + 6
    rec_j = lambda j: jnp.minimum(j, n_rec - 1)
    att_j = lambda j: jnp.maximum(j - n_rec, 0)
    return pl.pallas_call(
        functools.partial(_in_proj_kernel, n_rec=n_rec),
        grid=(bsz, s // tm, n_rec + n_att),
        in_specs=[
            pl.BlockSpec((None, tm, d), lambda b, i, j: (b, i, 0)),
            pl.BlockSpec((None, 1, d), lambda b, i, j: (layer, 0, 0)),
            pl.BlockSpec((None, d, tn), lambda b, i, j: (layer, 0, j)),
        ],
        out_specs=(
            pl.BlockSpec((None, tm, tn), lambda b, i, j: (b, i, rec_j(j))),
            pl.BlockSpec((None, None, None, tm, tn), lambda b, i, j: (att_j(j), b, 0, i, 0)),
            pl.BlockSpec((None, None, 4, tm // 4, tn), lambda b, i, j: (att_j(j), b, 0, i, 0)),
            pl.BlockSpec((None, None, 16, tm // 16, tn), lambda b, i, j: (att_j(j), b, 0, i, 0)),
        ),
        out_shape=(
            jax.ShapeDtypeStruct((bsz, s, n_rec * tn), BF16),
            jax.ShapeDtypeStruct((n_att, bsz, 1, s, tn), BF16),
            jax.ShapeDtypeStruct((n_att, bsz, 4, s // 4, tn), BF16),
            jax.ShapeDtypeStruct((n_att, bsz, 16, s // 16, tn), BF16),
        ),
        scratch_shapes=[pltpu.VMEM((tm, d), BF16),
                        pltpu.VMEM((n_slabs, tm, HEAD_DIM), F32),
                        pltpu.VMEM((n_slabs, tm, HEAD_DIM), F32)],
        compiler_params=_params(("parallel", "parallel", "arbitrary"), vmem),
        name="norm_in_proj",
    )(x, gain, w)


def _hgrn_tables():
    c = CHUNK
    t = np.arange(c)
    cum = (t[None, :] <= t[:, None]).astype(np.float32)
    exps = [cum]
    ups, masks = [], []
    half = c // 2
    while half >= 1:
        blk = 2 * half
        mid = (t // blk) * blk + half
        upper = (t % blk) >= half
        u = t[None, :]
        n_up = upper[:, None] & (u >= mid[:, None]) & (u <= t[:, None])
        n_lo = (~upper)[:, None] & (u >= t[:, None] + 1) & (u <= mid[:, None] - 1)
        exps.append((n_up | n_lo).astype(np.float32))
        ups.append(np.broadcast_to(upper[:, None], (c, HEAD_DIM)).astype(np.float32))
        same = (t[:, None] // blk) == (t[None, :] // blk)
        masks.append((upper[:, None] & (~upper)[None, :] & same).astype(np.float32))
        half //= 2
    masks.append(np.eye(c, dtype=np.float32))
    n_all = np.concatenate(exps, axis=0)
    n3 = np.concatenate([n_all, n_all, n_all], axis=1)
    return n3, np.stack(ups), np.stack(masks)


def _split3(x):
    hi = x.astype(BF16)
    r = x - hi.astype(F32)
    mid = r.astype(BF16)
    lo = (r - mid.astype(F32)).astype(BF16)
    return jnp.concatenate([hi, mid, lo], axis=0)


def _hgrn_kernel(q_ref, f_ref, i_ref, g_ref, lb_ref, nw_ref, n3_ref, up_ref, mask_ref,
                 o_ref, state_ref, e_ref):
    c = CHUNK
    n_levels = up_ref.shape[0]

    @pl.when(pl.program_id(1) == 0)
    def _():
        state_ref[...] = jnp.zeros_like(state_ref)

    n_e = n3_ref.shape[0]
    per_iter = e_ref.shape[0] // n_e
    heads = [slice(h * HEAD_DIM, (h + 1) * HEAD_DIM) for h in range(N_HEADS)]

    def body(it, carry):
        rows = pl.ds(pl.multiple_of(it * (per_iter * c), per_iter * c), per_iter * c)
        lb = lb_ref[...]
        q_all, i_all, g_all = q_ref[rows, :], i_ref[rows, :], g_ref[rows, :]
        f_all = lb + (1.0 - lb) * jax.nn.sigmoid(f_ref[rows, :].astype(F32))
        log2_f = jnp.log(f_all) * LOG2_E
        chains = []
        for u in range(per_iter):
            t = slice(u * c, (u + 1) * c)
            e_ref[u * n_e:(u + 1) * n_e, :] = _dot(n3_ref[...], _split3(log2_f[t]))
            for cols in heads:
                chains.append((u, t, cols))
        q = [jax.nn.silu(q_all[t, cols].astype(F32)) for _, t, cols in chains]
        k = [1.0 - f_all[t, cols] for _, t, cols in chains]
        scores = [_dot_nt(q[n].astype(BF16), k[n].astype(BF16)) * mask_ref[n_levels] for n in range(len(chains))]
        for l in range(n_levels):
            half = c >> (l + 1)
            for n, (u, t, cols) in enumerate(chains):
                if half % 8 == 0:
                    qk = jnp.concatenate(
                        [(q[n] if (r // half) % 2 else k[n])[r:r + half] for r in range(0, c, half)], axis=0)
                else:
                    qk = jnp.where(up_ref[l] > 0.5, q[n], k[n])
                e_rows = slice(u * n_e + (l + 1) * c, u * n_e + (l + 2) * c)
                x = (qk * jnp.exp2(e_ref[e_rows, cols])).astype(BF16)
                scores[n] = scores[n] + _dot_nt(x, x) * mask_ref[l]

        outs = []
        for n, (u, t, cols) in enumerate(chains):
            h = n % N_HEADS
            v = i_all[t, cols]
            b = e_ref[u * n_e:u * n_e + c, cols]
            b_last = b[c - 1:c]
            state = state_ref[h]
            o = _dot_nt((q[n] * jnp.exp2(b)).astype(BF16), state.astype(BF16))
            o = o + _dot(scores[n].astype(BF16), v)
            k_dec = (k[n] * jnp.exp2(b_last - b)).astype(BF16)
            state_ref[h] = state * jnp.exp2(b_last) + _dot_tn(v, k_dec)
            outs.append(o)
        for n, (u, t, cols) in enumerate(chains):
            o = outs[n]
            o = o * lax.rsqrt(jnp.mean(o * o, axis=-1, keepdims=True) + EPS)
            o = o * nw_ref[:, cols] * jax.nn.silu(g_all[t, cols].astype(F32))
            outs[n] = o.astype(o_ref.dtype)
        o_ref[rows, :] = jnp.concatenate(
            [jnp.concatenate(outs[u * N_HEADS:(u + 1) * N_HEADS], axis=-1) for u in range(per_iter)], axis=0)
        return carry

    lax.fori_loop(0, q_ref.shape[0] // (per_iter * c), body, 0)


def hgrn2(proj, lb_all, norm_w, layer, *, tt=256, chunks_per_iter=2):
    bsz, s, _ = proj.shape
    tt = min(tt, s)
    assert s % tt == 0 and tt % (chunks_per_iter * CHUNK) == 0
    n3, ups, masks = _hgrn_tables()
    stream = lambda j: pl.BlockSpec((None, tt, GROUP_WIDTH), lambda b, t: (b, t, j))
    vec = pl.BlockSpec((None, 1, GROUP_WIDTH), lambda b, t: (layer, 0, 0))
    const = lambda a: pl.BlockSpec(a.shape, lambda b, t: (0,) * a.ndim)
    return pl.pallas_call(
        _hgrn_kernel,
        grid=(bsz, s // tt),
        in_specs=[stream(0), stream(1), stream(2), stream(3), vec, vec,
                  const(n3), const(ups), const(masks)],
        out_specs=pl.BlockSpec((None, tt, GROUP_WIDTH), lambda b, t: (b, t, 0)),
        out_shape=jax.ShapeDtypeStruct((bsz, s, GROUP_WIDTH), BF16),
        scratch_shapes=[pltpu.VMEM((N_HEADS, HEAD_DIM, HEAD_DIM), F32),
                        pltpu.VMEM((chunks_per_iter * n3.shape[0], GROUP_WIDTH), F32)],
        compiler_params=_params(("parallel", "arbitrary"), 24),
        name="hgrn2",
    )(proj, proj, proj, proj, lb_all, norm_w,
      jnp.asarray(n3, BF16), jnp.asarray(ups, F32), jnp.asarray(masks, F32))


def _attn_kernel(*refs, merge, emit_lse):
    q_ref, kp_ref, kc_ref, vp_ref, vc_ref = refs[:5]
    refs = refs[5:]
    if merge:
        oin_ref, lin_ref = refs[:2]
        refs = refs[2:]
    o_ref = refs[0]
    refs = refs[1:]
    if emit_lse:
        lout_ref = refs[0]
        refs = refs[1:]
    if merge:
        osc_ref, lsc_ref = refs

    blk = ATT_BLOCK
    rows_per_step = q_ref.shape[0]
    n = pl.program_id(2)

    if merge:
        quarter = rows_per_step // 4
        for sub in range(4):
            for h in range(N_HEADS):
                osc_ref[h, pl.ds(sub, quarter, stride=4), :] = (
                    oin_ref[sub, :, h * HEAD_DIM:(h + 1) * HEAD_DIM].astype(F32))
            lsc_ref[pl.ds(sub, quarter, stride=4), :] = lin_ref[sub]

    qi = lax.broadcasted_iota(jnp.int32, (blk, 2 * blk), 0)
    kj = lax.broadcasted_iota(jnp.int32, (blk, 2 * blk), 1)
    band = (kj >= qi) & (kj <= qi + blk)
    bias_inner = jnp.where(band, 0.0, -jnp.inf).astype(F32)
    bias_first = jnp.where(band & ((n > 0) | (kj >= blk)), 0.0, -jnp.inf).astype(F32)
    scale = HEAD_DIM ** -0.5
    lane = lax.broadcasted_iota(jnp.int32, (blk, HEAD_DIM), 1)
    ones = jnp.ones((2 * blk, HEAD_DIM), BF16)

    for j in range(rows_per_step // blk):
        rows = slice(j * blk, (j + 1) * blk)
        bias = bias_first if j == 0 else bias_inner
        lse_tile = jnp.zeros((blk, HEAD_DIM), F32)
        for h in range(N_HEADS):
            cols = slice(h * HEAD_DIM, (h + 1) * HEAD_DIM)
            if j == 0:
                kh = jnp.concatenate([kp_ref[:, cols], kc_ref[0:blk, cols]], axis=0)
                vh = jnp.concatenate([vp_ref[:, cols], vc_ref[0:blk, cols]], axis=0)
            else:
                kh = kc_ref[(j - 1) * blk:(j + 1) * blk, cols]
                vh = vc_ref[(j - 1) * blk:(j + 1) * blk, cols]
            s = _dot_nt(q_ref[rows, cols], kh) + bias
            m = jnp.max(s, axis=-1, keepdims=True)
            p = jnp.exp2((s - m) * (scale * LOG2_E))
            o_den = _dot(p.astype(BF16), jnp.concatenate([vh, ones], axis=1))
            den = o_den[:, HEAD_DIM:]
            o = o_den[:, :HEAD_DIM] / den
            lse = jnp.broadcast_to(m * scale, (blk, HEAD_DIM)) + jnp.log(den)
            if merge:
                lse_prev = jnp.broadcast_to(lsc_ref[rows, h:h + 1], (blk, HEAD_DIM))
                top = jnp.maximum(lse, lse_prev)
                w_prev = jnp.exp(lse_prev - top)
                w_new = jnp.exp(lse - top)
                tot = w_prev + w_new
                o = (osc_ref[h, rows, :] * w_prev + o * w_new) / tot
                lse = top + jnp.log(tot)
            o_ref[rows, cols] = o.astype(o_ref.dtype)
            if emit_lse:
                lse_tile = jnp.where(lane == h, lse, lse_tile)
        if emit_lse:
            lout_ref[rows, :] = lse_tile


def dilated_attention(planes, *, rows_per_step=512):
    blk = ATT_BLOCK
    o_acc = lse_acc = None
    for bi in reversed(range(len(DILATIONS))):
        d = DILATIONS[bi]
        qkv = planes[bi]
        _, bsz, _, rows, _ = qkv.shape
        step = min(rows_per_step, rows)
        assert rows % step == 0 and step % blk == 0
        merge = o_acc is not None
        last = bi == 0
        cur = lambda col: pl.BlockSpec((None, None, None, step, GROUP_WIDTH),
                                       lambda b, r, n: (col, b, r, n, 0))
        prev = lambda col: pl.BlockSpec(
            (None, None, None, blk, GROUP_WIDTH),
            lambda b, r, n: (col, b, r, jnp.maximum(n * (step // blk) - 1, 0), 0))
        in_specs = [cur(0), prev(1), cur(1), prev(2), cur(2)]
        args = [qkv] * 5
        scratch = []
        if merge:
            assert DILATIONS[bi + 1] == 4 * d
            coarse = lambda width: pl.BlockSpec((None, 4, None, step // 4, width),
                                                lambda b, r, n: (b, 0, r, n, 0))
            in_specs += [coarse(GROUP_WIDTH), coarse(HEAD_DIM)]
            args += [o_acc.reshape(bsz, 4, d, rows // 4, GROUP_WIDTH),
                     lse_acc.reshape(bsz, 4, d, rows // 4, HEAD_DIM)]
            scratch = [pltpu.VMEM((N_HEADS, step, HEAD_DIM), F32), pltpu.VMEM((step, HEAD_DIM), F32)]
        o_spec = pl.BlockSpec((None, None, step, GROUP_WIDTH), lambda b, r, n: (b, r, n, 0))
        l_spec = pl.BlockSpec((None, None, step, HEAD_DIM), lambda b, r, n: (b, r, n, 0))
        o_shape = jax.ShapeDtypeStruct((bsz, d, rows, GROUP_WIDTH), BF16)
        l_shape = jax.ShapeDtypeStruct((bsz, d, rows, HEAD_DIM), F32)
        outs = pl.pallas_call(
            functools.partial(_attn_kernel, merge=merge, emit_lse=not last),
            grid=(bsz, d, rows // step),
            in_specs=in_specs,
            out_specs=o_spec if last else (o_spec, l_spec),
            out_shape=o_shape if last else (o_shape, l_shape),
            scratch_shapes=scratch,
            compiler_params=_params(("parallel", "parallel", "arbitrary"), 32),
            name=f"dilated_attn_d{d}",
        )(*args)
        o_acc, lse_acc = (outs, None) if last else outs
    bsz, _, s, _ = o_acc.shape
    return o_acc.reshape(bsz, s, GROUP_WIDTH)


def _out_proj_kernel(h_ref, a1_ref, a2_ref, w_ref, g_ref, o_ref):
    half = a1_ref.shape[1]
    y = _dot(a1_ref[...], w_ref[0:half, :]) + _dot(a2_ref[...], w_ref[half:2 * half, :])
    o_ref[...] = h_ref[...] + _rms(y, g_ref[...])


def out_proj(h, a1, a2, w, gain, layer, *, tm=512):
    m, d = h.shape
    ka = a1.shape[1]
    tm = min(tm, m)
    vmem = (4 * tm * d * 4 + 4 * tm * ka * 2 + 2 * 2 * ka * d * 2 + 2 * tm * d * 4) // MIB + 6
    return pl.pallas_call(
        _out_proj_kernel,
        grid=(m // tm,),
        in_specs=[
            pl.BlockSpec((tm, d), lambda i: (i, 0)),
            pl.BlockSpec((tm, ka), lambda i: (i, 0)),
            pl.BlockSpec((tm, ka), lambda i: (i, 0)),
            pl.BlockSpec((None, 2 * ka, d), lambda i: (layer, 0, 0)),
            pl.BlockSpec((None, 1, d), lambda i: (layer, 0, 0)),
        ],
        out_specs=pl.BlockSpec((tm, d), lambda i: (i, 0)),
        out_shape=jax.ShapeDtypeStruct((m, d), F32),
        compiler_params=_params(("parallel",), vmem),
        name="out_proj_norm_res",
    )(h, a1, a2, w, gain)


def _ffn_kernel(h_ref, halo_ref, g1_ref, wg_ref, wv_ref, cwg_ref, cwv_ref, cbg_ref, cbv_ref,
                wd_ref, g2_ref, o_ref, xn_ref, acc_ref, *, tiles_per_seq):
    i, j = pl.program_id(0), pl.program_id(1)
    tm = h_ref.shape[0]

    @pl.when(j == 0)
    def _():
        first = (i % tiles_per_seq) == 0
        halo = jnp.where(first, 0.0, _rms(halo_ref[...], g1_ref[...]))
        xn_ref[0:HALO, :] = halo.astype(BF16)
        xn_ref[HALO:HALO + tm, :] = _rms(h_ref[...], g1_ref[...]).astype(BF16)
        acc_ref[...] = jnp.zeros_like(acc_ref)

    xn = xn_ref[...]

    def conv(w_ref, cw_ref, cb_ref):
        u = _dot(xn, w_ref[...])
        cw = cw_ref[...]
        y = cb_ref[...] + cw[2:3] * u + cw[1:2] * pltpu.roll(u, 1, 0) + cw[0:1] * pltpu.roll(u, 2, 0)
        return y[HALO:HALO + tm]

    gate = conv(wg_ref, cwg_ref, cbg_ref)
    val = conv(wv_ref, cwv_ref, cbv_ref)
    mid = (jax.nn.gelu(gate, approximate=True) * val).astype(BF16)
    acc_ref[...] += _dot(mid, wd_ref[...])

    @pl.when(j == pl.num_programs(1) - 1)
    def _():
        o_ref[...] = h_ref[...] + _rms(acc_ref[...], g2_ref[...])


def conv_gated_mlp(h, seq_len, g1, w_up, conv_w, conv_b, w_down, g2, layer, *, tm=512, tf=512):
    m, d = h.shape
    f = w_down.shape[1]
    tm, tf = min(tm, seq_len), min(tf, f)
    nf = f // tf
    assert seq_len % tm == 0 and f % tf == 0 and tm % HALO == 0
    halo_blocks = tm // HALO
    vmem = (4 * tm * d * 4 + (tm + HALO) * d * 2 + tm * d * 4 + 3 * 2 * d * tf * 2
            + 3 * (tm + HALO) * tf * 4) // MIB + 8
    gain = pl.BlockSpec((None, 1, d), lambda i, j: (layer, 0, 0))
    return pl.pallas_call(
        functools.partial(_ffn_kernel, tiles_per_seq=seq_len // tm),
        grid=(m // tm, nf),
        in_specs=[
            pl.BlockSpec((tm, d), lambda i, j: (i, 0)),
            pl.BlockSpec((HALO, d), lambda i, j: (jnp.maximum(i * halo_blocks - 1, 0), 0)),
            gain,
            pl.BlockSpec((None, d, tf), lambda i, j: (layer, 0, j)),
            pl.BlockSpec((None, d, tf), lambda i, j: (layer, 0, nf + j)),
            pl.BlockSpec((None, CONV_WIDTH, tf), lambda i, j: (layer, 0, j)),
            pl.BlockSpec((None, CONV_WIDTH, tf), lambda i, j: (layer, 0, nf + j)),
            pl.BlockSpec((None, 1, tf), lambda i, j: (layer, 0, j)),
            pl.BlockSpec((None, 1, tf), lambda i, j: (layer, 0, nf + j)),
            pl.BlockSpec((None, tf, d), lambda i, j: (layer, j, 0)),
            gain,
        ],
        out_specs=pl.BlockSpec((tm, d), lambda i, j: (i, 0)),
        out_shape=jax.ShapeDtypeStruct((m, d), F32),
        scratch_shapes=[pltpu.VMEM((tm + HALO, d), BF16), pltpu.VMEM((tm, d), F32)],
        compiler_params=_params(("parallel", "arbitrary"), vmem),
        name="conv_gated_mlp",
    )(h, h, g1, w_up, w_up, conv_w, conv_w, conv_b, conv_b, w_down, g2)


def _ple_kernel(h_ref, p_ref, wpe_ref, wpg_ref, o_ref):
    h = h_ref[...]
    emb = _dot(p_ref[...].astype(BF16), wpe_ref[...])
    gate = jax.nn.sigmoid(_dot(h.astype(BF16), wpg_ref[...]))
    o_ref[...] = h + emb * gate


def per_layer_embedding(h, p, w_pe, w_pg, layer, *, tm=512):
    m, d = h.shape
    pd = p.shape[-1]
    tm = min(tm, m)
    vmem = (4 * tm * d * 4 + 2 * tm * pd * 4 + 2 * (pd + d) * d * 2 + 3 * tm * d * 4) // MIB + 6
    return pl.pallas_call(
        _ple_kernel,
        grid=(m // tm,),
        in_specs=[
            pl.BlockSpec((tm, d), lambda i: (i, 0)),
            pl.BlockSpec((None, tm, pd), lambda i: (layer, i, 0)),
            pl.BlockSpec((None, pd, d), lambda i: (layer, 0, 0)),
            pl.BlockSpec((None, d, d), lambda i: (layer, 0, 0)),
        ],
        out_specs=pl.BlockSpec((tm, d), lambda i: (i, 0)),
        out_shape=jax.ShapeDtypeStruct((m, d), F32),
        compiler_params=_params(("parallel",), vmem),
        name="per_layer_embedding",
    )(h, p, w_pe, w_pg)


def kernel(x, p, ln_mix_pre, w_in, lb_logits, hgrn_norm, w_out, ln_mix_post, ln_ffn_pre, w_up,
           conv_w, conv_b, w_down, ln_ffn_post, w_pe, w_pg):
    bsz, s, d = x.shape
    depth = w_in.shape[0]
    m = bsz * s
    row = lambda a: a.astype(F32).reshape(depth, 1, a.shape[-1])
    w_in_b, w_out_b, w_up_b, w_down_b = (w.astype(BF16) for w in (w_in, w_out, w_up, w_down))
    w_pe_b, w_pg_b = w_pe.astype(BF16), w_pg.astype(BF16)
    g_mix_pre, g_mix_post, g_ffn_pre, g_ffn_post = map(row, (ln_mix_pre, ln_mix_post, ln_ffn_pre, ln_ffn_post))
    hgrn_w, conv_bias = row(hgrn_norm), row(conv_b)
    lb_all = lower_bounds(lb_logits).reshape(depth, 1, GROUP_WIDTH)
    p2 = p.reshape(depth, m, p.shape[-1])

    h = x.reshape(m, d)
    for l in range(depth):
        rec, *planes = in_proj(h.reshape(bsz, s, d), g_mix_pre, w_in_b, l, 4, 3)
        o_rec = hgrn2(rec, lb_all, hgrn_w, l)
        o_att = dilated_attention(planes)
        h = out_proj(h, o_rec.reshape(m, GROUP_WIDTH), o_att.reshape(m, GROUP_WIDTH), w_out_b, g_mix_post, l)
        h = conv_gated_mlp(h, s, g_ffn_pre, w_up_b, conv_w, conv_bias, w_down_b, g_ffn_post, l)
        h = per_layer_embedding(h, p2, w_pe_b, w_pg_b, l)
    return h.reshape(bsz, s, d)
```

```python
import functools

import numpy as np
import jax
import jax.numpy as jnp
from jax import lax
from jax.experimental import pallas as pl
from jax.experimental.pallas import tpu as pltpu

F32 = jnp.float32
BF16 = jnp.bfloat16

EPS = 1e-6
LOG2_E = 1.4426950408889634
HEAD_DIM = 128
N_HEADS = 8
GROUP_WIDTH = N_HEADS * HEAD_DIM
N_STREAMS = 7
CHUNK = 64
ATT_BLOCK = 128
DILATIONS = (1, 4, 16)
CONV_WIDTH = 3
HALO = 16
MIB = 1024 * 1024
V7X_VMEM_CAP_MIB = 56


def _params(semantics, vmem_mib):
    return pltpu.CompilerParams(
        dimension_semantics=semantics,
        vmem_limit_bytes=min(vmem_mib, V7X_VMEM_CAP_MIB) * MIB,
    )


def _rms(x, gain):
    return x * lax.rsqrt(jnp.mean(x * x, axis=-1, keepdims=True) + EPS) * gain


def _dot(a, b):
    return jnp.dot(a, b, preferred_element_type=F32)


def _dot_nt(a, b):
    return lax.dot_general(a, b, (((1,), (1,)), ((), ())), preferred_element_type=F32)


def _dot_tn(a, b):
    return lax.dot_general(a, b, (((0,), (0,)), ((), ())), preferred_element_type=F32)


def _lb_kernel(logit_ref, o_ref):
    x = logit_ref[...]
    n_layers = x.shape[0]
    e = jnp.exp(x - jnp.max(x, axis=0, keepdims=True))
    sm = e / jnp.sum(e, axis=0, keepdims=True)
    run = sm[0:1]
    rows = [run]
    for l in range(1, n_layers):
        run = run + sm[l:l + 1]
        rows.append(run)
    for l in range(n_layers):
        o_ref[l:l + 1, :] = rows[l] - rows[0]


def lower_bounds(lb_logits):
    return pl.pallas_call(
        _lb_kernel,
        out_shape=jax.ShapeDtypeStruct(lb_logits.shape, F32),
        name="lower_bounds",
    )(lb_logits.astype(F32))


def _norm_matmul_kernel(x_ref, g_ref, w_ref, o_ref, xn_ref):
    @pl.when(pl.program_id(1) == 0)
    def _():
        xn_ref[...] = _rms(x_ref[...], g_ref[...]).astype(BF16)

    o_ref[...] = _dot(xn_ref[...], w_ref[...]).astype(o_ref.dtype)


def norm_matmul(x, gain, w, layer, n_cols, *, tm=1024, tn=1024):
    m, d = x.shape
    tm = min(tm, m)
    vmem = (2 * tm * d * 4 + tm * d * 2 + 2 * d * tn * 2 + 2 * tm * tn * 2 + tm * tn * 4) // MIB + 6
    return pl.pallas_call(
        _norm_matmul_kernel,
        grid=(m // tm, n_cols // tn),
        in_specs=[
            pl.BlockSpec((tm, d), lambda i, j: (i, 0)),
            pl.BlockSpec((None, 1, d), lambda i, j: (layer, 0, 0)),
            pl.BlockSpec((None, d, tn), lambda i, j: (layer, 0, j)),
        ],
        out_specs=pl.BlockSpec((tm, tn), lambda i, j: (i, j)),
        out_shape=jax.ShapeDtypeStruct((m, n_cols), BF16),
        scratch_shapes=[pltpu.VMEM((tm, d), BF16)],
        compiler_params=_params(("parallel", "arbitrary"), vmem),
        name="norm_in_proj",
    )(x, gain, w)


def _norm_matmul_planes_kernel(x_ref, g_ref, w_ref, o1_ref, o4_ref, o16_ref, xn_ref, nat_ref, pl4_ref):
    @pl.when(pl.program_id(2) == 0)
    def _():
        xn_ref[...] = _rms(x_ref[...], g_ref[...]).astype(BF16)

    res = _dot(xn_ref[...], w_ref[...])
    tm, tn = res.shape
    o1_ref[...] = res.astype(BF16)
    n_slabs = tn // HEAD_DIM
    q4, q16 = tm // 4, tm // 16
    for c in range(n_slabs):
        nat_ref[c] = res[:, c * HEAD_DIM:(c + 1) * HEAD_DIM]
    for c in range(n_slabs):
        cols = slice(c * HEAD_DIM, (c + 1) * HEAD_DIM)
        for r4 in range(4):
            plane = nat_ref[c, pl.ds(r4, q4, stride=4), :]
            o4_ref[r4, :, cols] = plane.astype(BF16)
            pl4_ref[c, r4 * q4:(r4 + 1) * q4, :] = plane
    for c in range(n_slabs):
        cols = slice(c * HEAD_DIM, (c + 1) * HEAD_DIM)
        for r4 in range(4):
            for sub in range(4):
                plane = pl4_ref[c, pl.ds(r4 * q4 + sub, q16, stride=4), :]
                o16_ref[4 * sub + r4, :, cols] = plane.astype(BF16)


def norm_matmul_planes(x, gain, w, layer, col0, n_streams, *, tm=512):
    bsz, s, d = x.shape
    tn = GROUP_WIDTH
    tm = min(tm, s)
    assert DILATIONS == (1, 4, 16)
    assert s % tm == 0 and tm % 256 == 0 and col0 % tn == 0
    n_slabs = tn // HEAD_DIM
    vmem = (2 * tm * d * 4 + tm * d * 2 + 2 * d * tn * 2 + 3 * 2 * tm * tn * 2 + 3 * tm * tn * 4) // MIB + 6
    return pl.pallas_call(
        _norm_matmul_planes_kernel,
        grid=(bsz, s // tm, n_streams),
        in_specs=[
            pl.BlockSpec((None, tm, d), lambda b, i, j: (b, i, 0)),
            pl.BlockSpec((None, 1, d), lambda b, i, j: (layer, 0, 0)),
            pl.BlockSpec((None, d, tn), lambda b, i, j: (layer, 0, col0 // tn + j)),
        ],
        out_specs=(
            pl.BlockSpec((None, None, None, tm, tn), lambda b, i, j: (j, b, 0, i, 0)),
            pl.BlockSpec((None, None, 4, tm // 4, tn), lambda b, i, j: (j, b, 0, i, 0)),
            pl.BlockSpec((None, None, 16, tm // 16, tn), lambda b, i, j: (j, b, 0, i, 0)),
        ),
        out_shape=(
            jax.ShapeDtypeStruct((n_streams, bsz, 1, s, tn), BF16),
            jax.ShapeDtypeStruct((n_streams, bsz, 4, s // 4, tn), BF16),
            jax.ShapeDtypeStruct((n_streams, bsz, 16, s // 16, tn), BF16),
        ),
        scratch_shapes=[pltpu.VMEM((tm, d), BF16),
                        pltpu.VMEM((n_slabs, tm, HEAD_DIM), F32),
                        pltpu.VMEM((n_slabs, tm, HEAD_DIM), F32)],
        compiler_params=_params(("parallel", "parallel", "arbitrary"), vmem),
        name="norm_in_proj_planes",
    )(x, gain, w)


def _hgrn_tables():
    c = CHUNK
    t = np.arange(c)
    cum = (t[None, :] <= t[:, None]).astype(np.float32)
    exps = [cum]
    ups, masks = [], []
    half = c // 2
    while half >= 1:
        blk = 2 * half
        mid = (t // blk) * blk + half
        upper = (t % blk) >= half
        u = t[None, :]
        n_up = upper[:, None] & (u >= mid[:, None]) & (u <= t[:, None])
        n_lo = (~upper)[:, None] & (u >= t[:, None] + 1) & (u <= mid[:, None] - 1)
        exps.append((n_up | n_lo).astype(np.float32))
        ups.append(np.broadcast_to(upper[:, None], (c, HEAD_DIM)).astype(np.float32))
        same = (t[:, None] // blk) == (t[None, :] // blk)
        masks.append((upper[:, None] & (~upper)[None, :] & same).astype(np.float32))
        half //= 2
    masks.append(np.eye(c, dtype=np.float32))
    n_all = np.concatenate(exps, axis=0)
    n3 = np.concatenate([n_all, n_all, n_all], axis=1)
    return n3, np.stack(ups), np.stack(masks)


def _split3(x):
    hi = x.astype(BF16)
    r = x - hi.astype(F32)
    mid = r.astype(BF16)
    lo = (r - mid.astype(F32)).astype(BF16)
    return jnp.concatenate([hi, mid, lo], axis=0)


def _hgrn_kernel(q_ref, f_ref, i_ref, g_ref, lb_ref, nw_ref, n3_ref, up_ref, mask_ref,
                 o_ref, state_ref, e_ref):
    c = CHUNK
    n_levels = up_ref.shape[0]

    @pl.when(pl.program_id(1) == 0)
    def _():
        state_ref[...] = jnp.zeros_like(state_ref)

    n_e = n3_ref.shape[0]
    per_iter = e_ref.shape[0] // n_e
    heads = [slice(h * HEAD_DIM, (h + 1) * HEAD_DIM) for h in range(N_HEADS)]

    def body(it, carry):
        rows = pl.ds(pl.multiple_of(it * (per_iter * c), per_iter * c), per_iter * c)
        lb = lb_ref[...]
        q_all, i_all, g_all = q_ref[rows, :], i_ref[rows, :], g_ref[rows, :]
        f_all = lb + (1.0 - lb) * jax.nn.sigmoid(f_ref[rows, :].astype(F32))
        log2_f = jnp.log(f_all) * LOG2_E
        chains = []
        for u in range(per_iter):
            t = slice(u * c, (u + 1) * c)
            e_ref[u * n_e:(u + 1) * n_e, :] = _dot(n3_ref[...], _split3(log2_f[t]))
            for cols in heads:
                chains.append((u, t, cols))
        q = [jax.nn.silu(q_all[t, cols].astype(F32)) for _, t, cols in chains]
        k = [1.0 - f_all[t, cols] for _, t, cols in chains]
        scores = [_dot_nt(q[n].astype(BF16), k[n].astype(BF16)) * mask_ref[n_levels] for n in range(len(chains))]
        for l in range(n_levels):
            half = c >> (l + 1)
            for n, (u, t, cols) in enumerate(chains):
                if half % 8 == 0:
                    qk = jnp.concatenate(
                        [(q[n] if (r // half) % 2 else k[n])[r:r + half] for r in range(0, c, half)], axis=0)
                else:
                    qk = jnp.where(up_ref[l] > 0.5, q[n], k[n])
                e_rows = slice(u * n_e + (l + 1) * c, u * n_e + (l + 2) * c)
                x = (qk * jnp.exp2(e_ref[e_rows, cols])).astype(BF16)
                scores[n] = scores[n] + _dot_nt(x, x) * mask_ref[l]

        outs = []
        for n, (u, t, cols) in enumerate(chains):
            h = n % N_HEADS
            v = i_all[t, cols]
            b = e_ref[u * n_e:u * n_e + c, cols]
            b_last = b[c - 1:c]
            state = state_ref[h]
            o = _dot_nt((q[n] * jnp.exp2(b)).astype(BF16), state.astype(BF16))
            o = o + _dot(scores[n].astype(BF16), v)
            k_dec = (k[n] * jnp.exp2(b_last - b)).astype(BF16)
            state_ref[h] = state * jnp.exp2(b_last) + _dot_tn(v, k_dec)
            outs.append(o)
        for n, (u, t, cols) in enumerate(chains):
            o = outs[n]
            o = o * lax.rsqrt(jnp.mean(o * o, axis=-1, keepdims=True) + EPS)
            o = o * nw_ref[:, cols] * jax.nn.silu(g_all[t, cols].astype(F32))
            outs[n] = o.astype(o_ref.dtype)
        o_ref[rows, :] = jnp.concatenate(
            [jnp.concatenate(outs[u * N_HEADS:(u + 1) * N_HEADS], axis=-1) for u in range(per_iter)], axis=0)
        return carry

    lax.fori_loop(0, q_ref.shape[0] // (per_iter * c), body, 0)


def hgrn2(proj, lb_all, norm_w, layer, *, tt=256, chunks_per_iter=2):
    bsz, s, _ = proj.shape
    tt = min(tt, s)
    assert s % tt == 0 and tt % (chunks_per_iter * CHUNK) == 0
    n3, ups, masks = _hgrn_tables()
    stream = lambda j: pl.BlockSpec((None, tt, GROUP_WIDTH), lambda b, t: (b, t, j))
    vec = pl.BlockSpec((None, 1, GROUP_WIDTH), lambda b, t: (layer, 0, 0))
    const = lambda a: pl.BlockSpec(a.shape, lambda b, t: (0,) * a.ndim)
    return pl.pallas_call(
        _hgrn_kernel,
        grid=(bsz, s // tt),
        in_specs=[stream(0), stream(1), stream(2), stream(3), vec, vec,
                  const(n3), const(ups), const(masks)],
        out_specs=pl.BlockSpec((None, tt, GROUP_WIDTH), lambda b, t: (b, t, 0)),
        out_shape=jax.ShapeDtypeStruct((bsz, s, GROUP_WIDTH), BF16),
        scratch_shapes=[pltpu.VMEM((N_HEADS, HEAD_DIM, HEAD_DIM), F32),
                        pltpu.VMEM((chunks_per_iter * n3.shape[0], GROUP_WIDTH), F32)],
        compiler_params=_params(("parallel", "arbitrary"), 24),
        name="hgrn2",
    )(proj, proj, proj, proj, lb_all, norm_w,
      jnp.asarray(n3, BF16), jnp.asarray(ups, F32), jnp.asarray(masks, F32))


def _attn_kernel(*refs, merge, emit_lse):
    q_ref, kp_ref, kc_ref, vp_ref, vc_ref = refs[:5]
    refs = refs[5:]
    if merge:
        oin_ref, lin_ref = refs[:2]
        refs = refs[2:]
    o_ref = refs[0]
    refs = refs[1:]
    if emit_lse:
        lout_ref = refs[0]
        refs = refs[1:]
    if merge:
        osc_ref, lsc_ref = refs

    blk = ATT_BLOCK
    rows_per_step = q_ref.shape[0]
    n = pl.program_id(2)

    if merge:
        quarter = rows_per_step // 4
        for sub in range(4):
            for h in range(N_HEADS):
                osc_ref[h, pl.ds(sub, quarter, stride=4), :] = (
                    oin_ref[sub, :, h * HEAD_DIM:(h + 1) * HEAD_DIM].astype(F32))
            lsc_ref[pl.ds(sub, quarter, stride=4), :] = lin_ref[sub]

    qi = lax.broadcasted_iota(jnp.int32, (blk, 2 * blk), 0)
    kj = lax.broadcasted_iota(jnp.int32, (blk, 2 * blk), 1)
    band = (kj >= qi) & (kj <= qi + blk)
    bias_inner = jnp.where(band, 0.0, -jnp.inf).astype(F32)
    bias_first = jnp.where(band & ((n > 0) | (kj >= blk)), 0.0, -jnp.inf).astype(F32)
    scale = HEAD_DIM ** -0.5
    lane = lax.broadcasted_iota(jnp.int32, (blk, HEAD_DIM), 1)
    ones = jnp.ones((2 * blk, HEAD_DIM), BF16)

    for j in range(rows_per_step // blk):
        rows = slice(j * blk, (j + 1) * blk)
        bias = bias_first if j == 0 else bias_inner
        lse_tile = jnp.zeros((blk, HEAD_DIM), F32)
        for h in range(N_HEADS):
            cols = slice(h * HEAD_DIM, (h + 1) * HEAD_DIM)
            if j == 0:
                kh = jnp.concatenate([kp_ref[:, cols], kc_ref[0:blk, cols]], axis=0)
                vh = jnp.concatenate([vp_ref[:, cols], vc_ref[0:blk, cols]], axis=0)
            else:
                kh = kc_ref[(j - 1) * blk:(j + 1) * blk, cols]
                vh = vc_ref[(j - 1) * blk:(j + 1) * blk, cols]
            s = _dot_nt(q_ref[rows, cols], kh) + bias
            m = jnp.max(s, axis=-1, keepdims=True)
            p = jnp.exp2((s - m) * (scale * LOG2_E))
            o_den = _dot(p.astype(BF16), jnp.concatenate([vh, ones], axis=1))
            den = o_den[:, HEAD_DIM:]
            o = o_den[:, :HEAD_DIM] / den
            lse = jnp.broadcast_to(m * scale, (blk, HEAD_DIM)) + jnp.log(den)
            if merge:
                lse_prev = jnp.broadcast_to(lsc_ref[rows, h:h + 1], (blk, HEAD_DIM))
                top = jnp.maximum(lse, lse_prev)
                w_prev = jnp.exp(lse_prev - top)
                w_new = jnp.exp(lse - top)
                tot = w_prev + w_new
                o = (osc_ref[h, rows, :] * w_prev + o * w_new) / tot
                lse = top + jnp.log(tot)
            o_ref[rows, cols] = o.astype(o_ref.dtype)
            if emit_lse:
                lse_tile = jnp.where(lane == h, lse, lse_tile)
        if emit_lse:
            lout_ref[rows, :] = lse_tile


def dilated_attention(planes, *, rows_per_step=512):
    blk = ATT_BLOCK
    o_acc = lse_acc = None
    for bi in reversed(range(len(DILATIONS))):
        d = DILATIONS[bi]
        qkv = planes[bi]
        _, bsz, _, rows, _ = qkv.shape
        step = min(rows_per_step, rows)
        assert rows % step == 0 and step % blk == 0
        merge = o_acc is not None
        last = bi == 0
        cur = lambda col: pl.BlockSpec((None, None, None, step, GROUP_WIDTH),
                                       lambda b, r, n: (col, b, r, n, 0))
        prev = lambda col: pl.BlockSpec(
            (None, None, None, blk, GROUP_WIDTH),
            lambda b, r, n: (col, b, r, jnp.maximum(n * (step // blk) - 1, 0), 0))
        in_specs = [cur(0), prev(1), cur(1), prev(2), cur(2)]
        args = [qkv] * 5
        scratch = []
        if merge:
            assert DILATIONS[bi + 1] == 4 * d
            coarse = lambda width: pl.BlockSpec((None, 4, None, step // 4, width),
                                                lambda b, r, n: (b, 0, r, n, 0))
            in_specs += [coarse(GROUP_WIDTH), coarse(HEAD_DIM)]
            args += [o_acc.reshape(bsz, 4, d, rows // 4, GROUP_WIDTH),
                     lse_acc.reshape(bsz, 4, d, rows // 4, HEAD_DIM)]
            scratch = [pltpu.VMEM((N_HEADS, step, HEAD_DIM), F32), pltpu.VMEM((step, HEAD_DIM), F32)]
        o_spec = pl.BlockSpec((None, None, step, GROUP_WIDTH), lambda b, r, n: (b, r, n, 0))
        l_spec = pl.BlockSpec((None, None, step, HEAD_DIM), lambda b, r, n: (b, r, n, 0))
        o_shape = jax.ShapeDtypeStruct((bsz, d, rows, GROUP_WIDTH), BF16)
        l_shape = jax.ShapeDtypeStruct((bsz, d, rows, HEAD_DIM), F32)
        outs = pl.pallas_call(
            functools.partial(_attn_kernel, merge=merge, emit_lse=not last),
            grid=(bsz, d, rows // step),
            in_specs=in_specs,
            out_specs=o_spec if last else (o_spec, l_spec),
            out_shape=o_shape if last else (o_shape, l_shape),
            scratch_shapes=scratch,
            compiler_params=_params(("parallel", "parallel", "arbitrary"), 32),
            name=f"dilated_attn_d{d}",
        )(*args)
        o_acc, lse_acc = (outs, None) if last else outs
    bsz, _, s, _ = o_acc.shape
    return o_acc.reshape(bsz, s, GROUP_WIDTH)


def _out_proj_kernel(h_ref, a1_ref, a2_ref, w_ref, g_ref, o_ref):
    half = a1_ref.shape[1]
    y = _dot(a1_ref[...], w_ref[0:half, :]) + _dot(a2_ref[...], w_ref[half:2 * half, :])
    o_ref[...] = h_ref[...] + _rms(y, g_ref[...])


def out_proj(h, a1, a2, w, gain, layer, *, tm=512):
    m, d = h.shape
    ka = a1.shape[1]
    tm = min(tm, m)
    vmem = (4 * tm * d * 4 + 4 * tm * ka * 2 + 2 * 2 * ka * d * 2 + 2 * tm * d * 4) // MIB + 6
    return pl.pallas_call(
        _out_proj_kernel,
        grid=(m // tm,),
        in_specs=[
            pl.BlockSpec((tm, d), lambda i: (i, 0)),
            pl.BlockSpec((tm, ka), lambda i: (i, 0)),
            pl.BlockSpec((tm, ka), lambda i: (i, 0)),
            pl.BlockSpec((None, 2 * ka, d), lambda i: (layer, 0, 0)),
            pl.BlockSpec((None, 1, d), lambda i: (layer, 0, 0)),
        ],
        out_specs=pl.BlockSpec((tm, d), lambda i: (i, 0)),
        out_shape=jax.ShapeDtypeStruct((m, d), F32),
        compiler_params=_params(("parallel",), vmem),
        name="out_proj_norm_res",
    )(h, a1, a2, w, gain)


def _ffn_kernel(h_ref, halo_ref, g1_ref, wg_ref, wv_ref, cwg_ref, cwv_ref, cbg_ref, cbv_ref,
                wd_ref, g2_ref, o_ref, xn_ref, acc_ref, *, tiles_per_seq):
    i, j = pl.program_id(0), pl.program_id(1)
    tm = h_ref.shape[0]

    @pl.when(j == 0)
    def _():
        first = (i % tiles_per_seq) == 0
        halo = jnp.where(first, 0.0, _rms(halo_ref[...], g1_ref[...]))
        xn_ref[0:HALO, :] = halo.astype(BF16)
        xn_ref[HALO:HALO + tm, :] = _rms(h_ref[...], g1_ref[...]).astype(BF16)
        acc_ref[...] = jnp.zeros_like(acc_ref)

    xn = xn_ref[...]

    def conv(w_ref, cw_ref, cb_ref):
        u = _dot(xn, w_ref[...])
        cw = cw_ref[...]
        y = cb_ref[...] + cw[2:3] * u + cw[1:2] * pltpu.roll(u, 1, 0) + cw[0:1] * pltpu.roll(u, 2, 0)
        return y[HALO:HALO + tm]

    gate = conv(wg_ref, cwg_ref, cbg_ref)
    val = conv(wv_ref, cwv_ref, cbv_ref)
    mid = (jax.nn.gelu(gate, approximate=True) * val).astype(BF16)
    acc_ref[...] += _dot(mid, wd_ref[...])

    @pl.when(j == pl.num_programs(1) - 1)
    def _():
        o_ref[...] = h_ref[...] + _rms(acc_ref[...], g2_ref[...])


def conv_gated_mlp(h, seq_len, g1, w_up, conv_w, conv_b, w_down, g2, layer, *, tm=512, tf=512):
    m, d = h.shape
    f = w_down.shape[1]
    tm, tf = min(tm, seq_len), min(tf, f)
    nf = f // tf
    assert seq_len % tm == 0 and f % tf == 0 and tm % HALO == 0
    halo_blocks = tm // HALO
    vmem = (4 * tm * d * 4 + (tm + HALO) * d * 2 + tm * d * 4 + 3 * 2 * d * tf * 2
            + 3 * (tm + HALO) * tf * 4) // MIB + 8
    gain = pl.BlockSpec((None, 1, d), lambda i, j: (layer, 0, 0))
    return pl.pallas_call(
        functools.partial(_ffn_kernel, tiles_per_seq=seq_len // tm),
        grid=(m // tm, nf),
        in_specs=[
            pl.BlockSpec((tm, d), lambda i, j: (i, 0)),
            pl.BlockSpec((HALO, d), lambda i, j: (jnp.maximum(i * halo_blocks - 1, 0), 0)),
            gain,
            pl.BlockSpec((None, d, tf), lambda i, j: (layer, 0, j)),
            pl.BlockSpec((None, d, tf), lambda i, j: (layer, 0, nf + j)),
            pl.BlockSpec((None, CONV_WIDTH, tf), lambda i, j: (layer, 0, j)),
            pl.BlockSpec((None, CONV_WIDTH, tf), lambda i, j: (layer, 0, nf + j)),
            pl.BlockSpec((None, 1, tf), lambda i, j: (layer, 0, j)),
            pl.BlockSpec((None, 1, tf), lambda i, j: (layer, 0, nf + j)),
            pl.BlockSpec((None, tf, d), lambda i, j: (layer, j, 0)),
            gain,
        ],
        out_specs=pl.BlockSpec((tm, d), lambda i, j: (i, 0)),
        out_shape=jax.ShapeDtypeStruct((m, d), F32),
        scratch_shapes=[pltpu.VMEM((tm + HALO, d), BF16), pltpu.VMEM((tm, d), F32)],
        compiler_params=_params(("parallel", "arbitrary"), vmem),
        name="conv_gated_mlp",
    )(h, h, g1, w_up, w_up, conv_w, conv_w, conv_b, conv_b, w_down, g2)


def _ple_kernel(h_ref, p_ref, wpe_ref, wpg_ref, o_ref):
    h = h_ref[...]
    emb = _dot(p_ref[...].astype(BF16), wpe_ref[...])
    gate = jax.nn.sigmoid(_dot(h.astype(BF16), wpg_ref[...]))
    o_ref[...] = h + emb * gate


def per_layer_embedding(h, p, w_pe, w_pg, layer, *, tm=512):
    m, d = h.shape
    pd = p.shape[-1]
    tm = min(tm, m)
    vmem = (4 * tm * d * 4 + 2 * tm * pd * 4 + 2 * (pd + d) * d * 2 + 3 * tm * d * 4) // MIB + 6
    return pl.pallas_call(
        _ple_kernel,
        grid=(m // tm,),
        in_specs=[
            pl.BlockSpec((tm, d), lambda i: (i, 0)),
            pl.BlockSpec((None, tm, pd), lambda i: (layer, i, 0)),
            pl.BlockSpec((None, pd, d), lambda i: (layer, 0, 0)),
            pl.BlockSpec((None, d, d), lambda i: (layer, 0, 0)),
        ],
        out_specs=pl.BlockSpec((tm, d), lambda i: (i, 0)),
        out_shape=jax.ShapeDtypeStruct((m, d), F32),
        compiler_params=_params(("parallel",), vmem),
        name="per_layer_embedding",
    )(h, p, w_pe, w_pg)


def kernel(x, p, ln_mix_pre, w_in, lb_logits, hgrn_norm, w_out, ln_mix_post, ln_ffn_pre, w_up,
           conv_w, conv_b, w_down, ln_ffn_post, w_pe, w_pg):
    bsz, s, d = x.shape
    depth = w_in.shape[0]
    m = bsz * s
    row = lambda a: a.astype(F32).reshape(depth, 1, a.shape[-1])
    w_in_b, w_out_b, w_up_b, w_down_b = (w.astype(BF16) for w in (w_in, w_out, w_up, w_down))
    w_pe_b, w_pg_b = w_pe.astype(BF16), w_pg.astype(BF16)
    g_mix_pre, g_mix_post, g_ffn_pre, g_ffn_post = map(row, (ln_mix_pre, ln_mix_post, ln_ffn_pre, ln_ffn_post))
    hgrn_w, conv_bias = row(hgrn_norm), row(conv_b)
    lb_all = lower_bounds(lb_logits).reshape(depth, 1, GROUP_WIDTH)
    p2 = p.reshape(depth, m, p.shape[-1])

    h = x.reshape(m, d)
    for l in range(depth):
        n_rec = 4 * GROUP_WIDTH
        rec = norm_matmul(h, g_mix_pre, w_in_b, l, n_rec).reshape(bsz, s, n_rec)
        planes = norm_matmul_planes(h.reshape(bsz, s, d), g_mix_pre, w_in_b, l, n_rec, 3)
        o_rec = hgrn2(rec, lb_all, hgrn_w, l)
        o_att = dilated_attention(planes)
        h = out_proj(h, o_rec.reshape(m, GROUP_WIDTH), o_att.reshape(m, GROUP_WIDTH), w_out_b, g_mix_post, l)
        h = conv_gated_mlp(h, s, g_ffn_pre, w_up_b, conv_w, conv_bias, w_down_b, g_ffn_post, l)
        h = per_layer_embedding(h, p2, w_pe_b, w_pg_b, l)
    return h.reshape(bsz, s, d)
```

```python
import functools

import numpy as np
import jax
import jax.numpy as jnp
from jax import lax
from jax.experimental import pallas as pl
from jax.experimental.pallas import tpu as pltpu

F32 = jnp.float32
BF16 = jnp.bfloat16

EPS = 1e-6
LOG2_E = 1.4426950408889634
HEAD_DIM = 128
N_HEADS = 8
GROUP_WIDTH = N_HEADS * HEAD_DIM
N_STREAMS = 7
CHUNK = 64
ATT_BLOCK = 128
DILATIONS = (1, 4, 16)
CONV_WIDTH = 3
CARRY = 8
MIB = 1024 * 1024
V7X_VMEM_CAP_MIB = 56


def _params(semantics, vmem_mib):
    return pltpu.CompilerParams(
        dimension_semantics=semantics,
        vmem_limit_bytes=min(vmem_mib, V7X_VMEM_CAP_MIB) * MIB,
    )


def _rms(x, gain):
    return x * lax.rsqrt(jnp.mean(x * x, axis=-1, keepdims=True) + EPS) * gain


def _dot(a, b):
    return jnp.dot(a, b, preferred_element_type=F32)


def _dot_nt(a, b):
    return lax.dot_general(a, b, (((1,), (1,)), ((), ())), preferred_element_type=F32)


def _dot_tn(a, b):
    return lax.dot_general(a, b, (((0,), (0,)), ((), ())), preferred_element_type=F32)


def _lb_kernel(logit_ref, o_ref):
    x = logit_ref[...]
    n_layers = x.shape[0]
    e = jnp.exp(x - jnp.max(x, axis=0, keepdims=True))
    sm = e / jnp.sum(e, axis=0, keepdims=True)
    run = sm[0:1]
    rows = [run]
    for l in range(1, n_layers):
        run = run + sm[l:l + 1]
        rows.append(run)
    for l in range(n_layers):
        o_ref[l:l + 1, :] = rows[l] - rows[0]


def lower_bounds(lb_logits):
    return pl.pallas_call(
        _lb_kernel,
        out_shape=jax.ShapeDtypeStruct(lb_logits.shape, F32),
        name="lower_bounds",
    )(lb_logits.astype(F32))


def _norm_matmul_kernel(x_ref, g_ref, w_ref, o_ref, xn_ref):
    @pl.when(pl.program_id(1) == 0)
    def _():
        xn_ref[...] = _rms(x_ref[...], g_ref[...]).astype(BF16)

    o_ref[...] = _dot(xn_ref[...], w_ref[...]).astype(o_ref.dtype)


def norm_matmul(x, gain, w, layer, n_cols, *, tm=1024, tn=1024):
    m, d = x.shape
    tm = min(tm, m)
    vmem = (2 * tm * d * 4 + tm * d * 2 + 2 * d * tn * 2 + 2 * tm * tn * 2 + tm * tn * 4) // MIB + 6
    return pl.pallas_call(
        _norm_matmul_kernel,
        grid=(m // tm, n_cols // tn),
        in_specs=[
            pl.BlockSpec((tm, d), lambda i, j: (i, 0)),
            pl.BlockSpec((None, 1, d), lambda i, j: (layer, 0, 0)),
            pl.BlockSpec((None, d, tn), lambda i, j: (layer, 0, j)),
        ],
        out_specs=pl.BlockSpec((tm, tn), lambda i, j: (i, j)),
        out_shape=jax.ShapeDtypeStruct((m, n_cols), BF16),
        scratch_shapes=[pltpu.VMEM((tm, d), BF16)],
        compiler_params=_params(("parallel", "arbitrary"), vmem),
        name="norm_in_proj",
    )(x, gain, w)


def _norm_matmul_planes_kernel(x_ref, g_ref, w_ref, o1_ref, o4_ref, o16_ref, xn_ref, nat_ref, pl4_ref):
    @pl.when(pl.program_id(2) == 0)
    def _():
        xn_ref[...] = _rms(x_ref[...], g_ref[...]).astype(BF16)

    res = _dot(xn_ref[...], w_ref[...])
    tm, tn = res.shape
    o1_ref[...] = res.astype(BF16)
    n_slabs = tn // HEAD_DIM
    q4, q16 = tm // 4, tm // 16
    for c in range(n_slabs):
        nat_ref[c] = res[:, c * HEAD_DIM:(c + 1) * HEAD_DIM]
    for c in range(n_slabs):
        cols = slice(c * HEAD_DIM, (c + 1) * HEAD_DIM)
        for r4 in range(4):
            plane = nat_ref[c, pl.ds(r4, q4, stride=4), :]
            o4_ref[r4, :, cols] = plane.astype(BF16)
            pl4_ref[c, r4 * q4:(r4 + 1) * q4, :] = plane
    for c in range(n_slabs):
        cols = slice(c * HEAD_DIM, (c + 1) * HEAD_DIM)
        for r4 in range(4):
            for sub in range(4):
                plane = pl4_ref[c, pl.ds(r4 * q4 + sub, q16, stride=4), :]
                o16_ref[4 * sub + r4, :, cols] = plane.astype(BF16)


def norm_matmul_planes(x, gain, w, layer, col0, n_streams, *, tm=512):
    bsz, s, d = x.shape
    tn = GROUP_WIDTH
    tm = min(tm, s)
    assert DILATIONS == (1, 4, 16)
    assert s % tm == 0 and tm % 256 == 0 and col0 % tn == 0
    n_slabs = tn // HEAD_DIM
    vmem = (2 * tm * d * 4 + tm * d * 2 + 2 * d * tn * 2 + 3 * 2 * tm * tn * 2 + 3 * tm * tn * 4) // MIB + 6
    return pl.pallas_call(
        _norm_matmul_planes_kernel,
        grid=(bsz, s // tm, n_streams),
        in_specs=[
            pl.BlockSpec((None, tm, d), lambda b, i, j: (b, i, 0)),
            pl.BlockSpec((None, 1, d), lambda b, i, j: (layer, 0, 0)),
            pl.BlockSpec((None, d, tn), lambda b, i, j: (layer, 0, col0 // tn + j)),
        ],
        out_specs=(
            pl.BlockSpec((None, None, None, tm, tn), lambda b, i, j: (j, b, 0, i, 0)),
            pl.BlockSpec((None, None, 4, tm // 4, tn), lambda b, i, j: (j, b, 0, i, 0)),
            pl.BlockSpec((None, None, 16, tm // 16, tn), lambda b, i, j: (j, b, 0, i, 0)),
        ),
        out_shape=(
            jax.ShapeDtypeStruct((n_streams, bsz, 1, s, tn), BF16),
            jax.ShapeDtypeStruct((n_streams, bsz, 4, s // 4, tn), BF16),
            jax.ShapeDtypeStruct((n_streams, bsz, 16, s // 16, tn), BF16),
        ),
        scratch_shapes=[pltpu.VMEM((tm, d), BF16),
                        pltpu.VMEM((n_slabs, tm, HEAD_DIM), F32),
                        pltpu.VMEM((n_slabs, tm, HEAD_DIM), F32)],
        compiler_params=_params(("parallel", "parallel", "arbitrary"), vmem),
        name="norm_in_proj_planes",
    )(x, gain, w)


def _hgrn_tables():
    c = CHUNK
    t = np.arange(c)
    cum = (t[None, :] <= t[:, None]).astype(np.float32)
    exps = [cum]
    ups, masks = [], []
    half = c // 2
    while half >= 1:
        blk = 2 * half
        mid = (t // blk) * blk + half
        upper = (t % blk) >= half
        u = t[None, :]
        n_up = upper[:, None] & (u >= mid[:, None]) & (u <= t[:, None])
        n_lo = (~upper)[:, None] & (u >= t[:, None] + 1) & (u <= mid[:, None] - 1)
        exps.append((n_up | n_lo).astype(np.float32))
        ups.append(np.broadcast_to(upper[:, None], (c, HEAD_DIM)).astype(np.float32))
        same = (t[:, None] // blk) == (t[None, :] // blk)
        masks.append((upper[:, None] & (~upper)[None, :] & same).astype(np.float32))
        half //= 2
    masks.append(np.eye(c, dtype=np.float32))
    n_all = np.concatenate(exps, axis=0)
    n3 = np.concatenate([n_all, n_all, n_all], axis=1)
    return n3, np.stack(ups), np.stack(masks)


def _split3(x):
    hi = x.astype(BF16)
    r = x - hi.astype(F32)
    mid = r.astype(BF16)
    lo = (r - mid.astype(F32)).astype(BF16)
    return jnp.concatenate([hi, mid, lo], axis=0)


def _hgrn_kernel(q_ref, f_ref, i_ref, g_ref, lb_ref, nw_ref, n3_ref, up_ref, mask_ref,
                 o_ref, state_ref, e_ref):
    c = CHUNK
    n_levels = up_ref.shape[0]

    @pl.when(pl.program_id(1) == 0)
    def _():
        state_ref[...] = jnp.zeros_like(state_ref)

    n_e = n3_ref.shape[0]
    per_iter = e_ref.shape[0] // n_e
    heads = [slice(h * HEAD_DIM, (h + 1) * HEAD_DIM) for h in range(N_HEADS)]

    def body(it, carry):
        rows = pl.ds(pl.multiple_of(it * (per_iter * c), per_iter * c), per_iter * c)
        lb = lb_ref[...]
        q_all, i_all, g_all = q_ref[rows, :], i_ref[rows, :], g_ref[rows, :]
        f_all = lb + (1.0 - lb) * jax.nn.sigmoid(f_ref[rows, :].astype(F32))
        log2_f = jnp.log(f_all) * LOG2_E
        chains = []
        for u in range(per_iter):
            t = slice(u * c, (u + 1) * c)
            e_ref[u * n_e:(u + 1) * n_e, :] = _dot(n3_ref[...], _split3(log2_f[t]))
            for cols in heads:
                chains.append((u, t, cols))
        q = [jax.nn.silu(q_all[t, cols].astype(F32)) for _, t, cols in chains]
        k = [1.0 - f_all[t, cols] for _, t, cols in chains]
        scores = [_dot_nt(q[n].astype(BF16), k[n].astype(BF16)) * mask_ref[n_levels] for n in range(len(chains))]
        for l in range(n_levels):
            half = c >> (l + 1)
            for n, (u, t, cols) in enumerate(chains):
                if half % 8 == 0:
                    qk = jnp.concatenate(
                        [(q[n] if (r // half) % 2 else k[n])[r:r + half] for r in range(0, c, half)], axis=0)
                else:
                    qk = jnp.where(up_ref[l] > 0.5, q[n], k[n])
                e_rows = slice(u * n_e + (l + 1) * c, u * n_e + (l + 2) * c)
                x = (qk * jnp.exp2(e_ref[e_rows, cols])).astype(BF16)
                scores[n] = scores[n] + _dot_nt(x, x) * mask_ref[l]

        outs = []
        for n, (u, t, cols) in enumerate(chains):
            h = n % N_HEADS
            v = i_all[t, cols]
            b = e_ref[u * n_e:u * n_e + c, cols]
            b_last = b[c - 1:c]
            state = state_ref[h]
            o = _dot_nt((q[n] * jnp.exp2(b)).astype(BF16), state.astype(BF16))
            o = o + _dot(scores[n].astype(BF16), v)
            k_dec = (k[n] * jnp.exp2(b_last - b)).astype(BF16)
            state_ref[h] = state * jnp.exp2(b_last) + _dot_tn(v, k_dec)
            outs.append(o)
        for n, (u, t, cols) in enumerate(chains):
            o = outs[n]
            o = o * lax.rsqrt(jnp.mean(o * o, axis=-1, keepdims=True) + EPS)
            o = o * nw_ref[:, cols] * jax.nn.silu(g_all[t, cols].astype(F32))
            outs[n] = o.astype(o_ref.dtype)
        o_ref[rows, :] = jnp.concatenate(
            [jnp.concatenate(outs[u * N_HEADS:(u + 1) * N_HEADS], axis=-1) for u in range(per_iter)], axis=0)
        return carry

    lax.fori_loop(0, q_ref.shape[0] // (per_iter * c), body, 0)


def hgrn2(proj, lb_all, norm_w, layer, *, tt=512, chunks_per_iter=4):
    bsz, s, _ = proj.shape
    tt = min(tt, s)
    assert s % tt == 0 and tt % (chunks_per_iter * CHUNK) == 0
    n3, ups, masks = _hgrn_tables()
    stream = lambda j: pl.BlockSpec((None, tt, GROUP_WIDTH), lambda b, t: (b, t, j))
    vec = pl.BlockSpec((None, 1, GROUP_WIDTH), lambda b, t: (layer, 0, 0))
    const = lambda a: pl.BlockSpec(a.shape, lambda b, t: (0,) * a.ndim)
    return pl.pallas_call(
        _hgrn_kernel,
        grid=(bsz, s // tt),
        in_specs=[stream(0), stream(1), stream(2), stream(3), vec, vec,
                  const(n3), const(ups), const(masks)],
        out_specs=pl.BlockSpec((None, tt, GROUP_WIDTH), lambda b, t: (b, t, 0)),
        out_shape=jax.ShapeDtypeStruct((bsz, s, GROUP_WIDTH), BF16),
        scratch_shapes=[pltpu.VMEM((N_HEADS, HEAD_DIM, HEAD_DIM), F32),
                        pltpu.VMEM((chunks_per_iter * n3.shape[0], GROUP_WIDTH), F32)],
        compiler_params=_params(("parallel", "arbitrary"), 24),
        name="hgrn2",
    )(proj, proj, proj, proj, lb_all, norm_w,
      jnp.asarray(n3, BF16), jnp.asarray(ups, F32), jnp.asarray(masks, F32))


def _attn_kernel(*refs, merge, emit_lse):
    q_ref, kp_ref, kc_ref, vp_ref, vc_ref = refs[:5]
    refs = refs[5:]
    if merge:
        oin_ref, lin_ref = refs[:2]
        refs = refs[2:]
    o_ref = refs[0]
    refs = refs[1:]
    if emit_lse:
        lout_ref = refs[0]
        refs = refs[1:]
    if merge:
        osc_ref, lsc_ref = refs

    blk = ATT_BLOCK
    rows_per_step = q_ref.shape[0]
    n = pl.program_id(2)

    if merge:
        quarter = rows_per_step // 4
        for sub in range(4):
            for h in range(N_HEADS):
                osc_ref[h, pl.ds(sub, quarter, stride=4), :] = (
                    oin_ref[sub, :, h * HEAD_DIM:(h + 1) * HEAD_DIM].astype(F32))
            lsc_ref[pl.ds(sub, quarter, stride=4), :] = lin_ref[sub]

    qi = lax.broadcasted_iota(jnp.int32, (blk, 2 * blk), 0)
    kj = lax.broadcasted_iota(jnp.int32, (blk, 2 * blk), 1)
    band = (kj >= qi) & (kj <= qi + blk)
    bias_inner = jnp.where(band, 0.0, -jnp.inf).astype(F32)
    bias_first = jnp.where(band & ((n > 0) | (kj >= blk)), 0.0, -jnp.inf).astype(F32)
    scale = HEAD_DIM ** -0.5
    lane = lax.broadcasted_iota(jnp.int32, (blk, HEAD_DIM), 1)
    ones = jnp.ones((2 * blk, HEAD_DIM), BF16)

    for j in range(rows_per_step // blk):
        rows = slice(j * blk, (j + 1) * blk)
        bias = bias_first if j == 0 else bias_inner
        lse_tile = jnp.zeros((blk, HEAD_DIM), F32)
        for h in range(N_HEADS):
            cols = slice(h * HEAD_DIM, (h + 1) * HEAD_DIM)
            if j == 0:
                kh = jnp.concatenate([kp_ref[:, cols], kc_ref[0:blk, cols]], axis=0)
                vh = jnp.concatenate([vp_ref[:, cols], vc_ref[0:blk, cols]], axis=0)
            else:
                kh = kc_ref[(j - 1) * blk:(j + 1) * blk, cols]
                vh = vc_ref[(j - 1) * blk:(j + 1) * blk, cols]
            s = _dot_nt(q_ref[rows, cols], kh) + bias
            m = jnp.max(s, axis=-1, keepdims=True)
            p = jnp.exp2((s - m) * (scale * LOG2_E))
            o_den = _dot(p.astype(BF16), jnp.concatenate([vh, ones], axis=1))
            den = o_den[:, HEAD_DIM:]
            o = o_den[:, :HEAD_DIM] / den
            lse = jnp.broadcast_to(m * scale, (blk, HEAD_DIM)) + jnp.log(den)
            if merge:
                lse_prev = jnp.broadcast_to(lsc_ref[rows, h:h + 1], (blk, HEAD_DIM))
                top = jnp.maximum(lse, lse_prev)
                w_prev = jnp.exp(lse_prev - top)
                w_new = jnp.exp(lse - top)
                tot = w_prev + w_new
                o = (osc_ref[h, rows, :] * w_prev + o * w_new) / tot
                lse = top + jnp.log(tot)
            o_ref[rows, cols] = o.astype(o_ref.dtype)
            if emit_lse:
                lse_tile = jnp.where(lane == h, lse, lse_tile)
        if emit_lse:
            lout_ref[rows, :] = lse_tile


def dilated_attention(planes, *, rows_per_step=1024):
    blk = ATT_BLOCK
    o_acc = lse_acc = None
    for bi in reversed(range(len(DILATIONS))):
        d = DILATIONS[bi]
        qkv = planes[bi]
        _, bsz, _, rows, _ = qkv.shape
        step = min(rows_per_step, rows)
        assert rows % step == 0 and step % blk == 0
        merge = o_acc is not None
        last = bi == 0
        cur = lambda col: pl.BlockSpec((None, None, None, step, GROUP_WIDTH),
                                       lambda b, r, n: (col, b, r, n, 0))
        prev = lambda col: pl.BlockSpec(
            (None, None, None, blk, GROUP_WIDTH),
            lambda b, r, n: (col, b, r, jnp.maximum(n * (step // blk) - 1, 0), 0))
        in_specs = [cur(0), prev(1), cur(1), prev(2), cur(2)]
        args = [qkv] * 5
        scratch = []
        if merge:
            assert DILATIONS[bi + 1] == 4 * d
            coarse = lambda width: pl.BlockSpec((None, 4, None, step // 4, width),
                                                lambda b, r, n: (b, 0, r, n, 0))
            in_specs += [coarse(GROUP_WIDTH), coarse(HEAD_DIM)]
            args += [o_acc.reshape(bsz, 4, d, rows // 4, GROUP_WIDTH),
                     lse_acc.reshape(bsz, 4, d, rows // 4, HEAD_DIM)]
            scratch = [pltpu.VMEM((N_HEADS, step, HEAD_DIM), F32), pltpu.VMEM((step, HEAD_DIM), F32)]
        vmem = (12 * step * GROUP_WIDTH * 2 + 3 * step * GROUP_WIDTH * 4) // MIB + 10
        o_spec = pl.BlockSpec((None, None, step, GROUP_WIDTH), lambda b, r, n: (b, r, n, 0))
        l_spec = pl.BlockSpec((None, None, step, HEAD_DIM), lambda b, r, n: (b, r, n, 0))
        o_shape = jax.ShapeDtypeStruct((bsz, d, rows, GROUP_WIDTH), BF16)
        l_shape = jax.ShapeDtypeStruct((bsz, d, rows, HEAD_DIM), F32)
        outs = pl.pallas_call(
            functools.partial(_attn_kernel, merge=merge, emit_lse=not last),
            grid=(bsz, d, rows // step),
            in_specs=in_specs,
            out_specs=o_spec if last else (o_spec, l_spec),
            out_shape=o_shape if last else (o_shape, l_shape),
            scratch_shapes=scratch,
            compiler_params=_params(("parallel", "parallel", "arbitrary"), vmem),
            name=f"dilated_attn_d{d}",
        )(*args)
        o_acc, lse_acc = (outs, None) if last else outs
    bsz, _, s, _ = o_acc.shape
    return o_acc.reshape(bsz, s, GROUP_WIDTH)


def _out_proj_kernel(h_ref, a1_ref, a2_ref, w_ref, g_ref, o_ref):
    half = a1_ref.shape[1]
    y = _dot(a1_ref[...], w_ref[0:half, :]) + _dot(a2_ref[...], w_ref[half:2 * half, :])
    o_ref[...] = h_ref[...] + _rms(y, g_ref[...])


def out_proj(h, a1, a2, w, gain, layer, *, tm=512):
    m, d = h.shape
    ka = a1.shape[1]
    tm = min(tm, m)
    vmem = (4 * tm * d * 4 + 4 * tm * ka * 2 + 2 * 2 * ka * d * 2 + 2 * tm * d * 4) // MIB + 6
    return pl.pallas_call(
        _out_proj_kernel,
        grid=(m // tm,),
        in_specs=[
            pl.BlockSpec((tm, d), lambda i: (i, 0)),
            pl.BlockSpec((tm, ka), lambda i: (i, 0)),
            pl.BlockSpec((tm, ka), lambda i: (i, 0)),
            pl.BlockSpec((None, 2 * ka, d), lambda i: (layer, 0, 0)),
            pl.BlockSpec((None, 1, d), lambda i: (layer, 0, 0)),
        ],
        out_specs=pl.BlockSpec((tm, d), lambda i: (i, 0)),
        out_shape=jax.ShapeDtypeStruct((m, d), F32),
        compiler_params=_params(("parallel",), vmem),
        name="out_proj_norm_res",
    )(h, a1, a2, w, gain)


def _ffn_kernel(h_ref, g1_ref, wg_ref, wv_ref, cwg_ref, cwv_ref, cbg_ref, cbv_ref,
                wd_ref, g2_ref, o_ref, xn_ref, acc_ref, tail_ref, *, tiles_per_seq):
    i, j = pl.program_id(0), pl.program_id(1)
    tm = h_ref.shape[0]

    @pl.when(j == 0)
    def _():
        xn_ref[...] = _rms(h_ref[...], g1_ref[...]).astype(BF16)
        acc_ref[...] = jnp.zeros_like(acc_ref)

    @pl.when((i == 0) & (j == 0))
    def _():
        tail_ref[...] = jnp.zeros_like(tail_ref)

    xn = xn_ref[...]
    first = (i % tiles_per_seq) == 0

    def conv(w_ref, cw_ref, cb_ref, slot):
        u = _dot(xn, w_ref[...])
        above = jnp.where(first, 0.0, tail_ref[slot, j])
        tail_ref[slot, j] = u[tm - CARRY:tm]
        u = jnp.concatenate([above, u], axis=0)
        cw = cw_ref[...]
        y = cb_ref[...] + cw[2:3] * u + cw[1:2] * pltpu.roll(u, 1, 0) + cw[0:1] * pltpu.roll(u, 2, 0)
        return y[CARRY:CARRY + tm]

    gate = conv(wg_ref, cwg_ref, cbg_ref, 0)
    val = conv(wv_ref, cwv_ref, cbv_ref, 1)
    mid = (jax.nn.gelu(gate, approximate=True) * val).astype(BF16)
    acc_ref[...] += _dot(mid, wd_ref[...])

    @pl.when(j == pl.num_programs(1) - 1)
    def _():
        o_ref[...] = h_ref[...] + _rms(acc_ref[...], g2_ref[...])


def conv_gated_mlp(h, seq_len, g1, w_up, conv_w, conv_b, w_down, g2, layer, *, tm=512, tf=512):
    m, d = h.shape
    f = w_down.shape[1]
    tm, tf = min(tm, seq_len), min(tf, f)
    nf = f // tf
    assert seq_len % tm == 0 and f % tf == 0 and tm % CARRY == 0 and CARRY >= CONV_WIDTH - 1
    vmem = (4 * tm * d * 4 + tm * d * 2 + tm * d * 4 + 3 * 2 * d * tf * 2
            + 3 * (tm + CARRY) * tf * 4 + 2 * nf * CARRY * tf * 4) // MIB + 8
    gain = pl.BlockSpec((None, 1, d), lambda i, j: (layer, 0, 0))
    return pl.pallas_call(
        functools.partial(_ffn_kernel, tiles_per_seq=seq_len // tm),
        grid=(m // tm, nf),
        in_specs=[
            pl.BlockSpec((tm, d), lambda i, j: (i, 0)),
            gain,
            pl.BlockSpec((None, d, tf), lambda i, j: (layer, 0, j)),
            pl.BlockSpec((None, d, tf), lambda i, j: (layer, 0, nf + j)),
            pl.BlockSpec((None, CONV_WIDTH, tf), lambda i, j: (layer, 0, j)),
            pl.BlockSpec((None, CONV_WIDTH, tf), lambda i, j: (layer, 0, nf + j)),
            pl.BlockSpec((None, 1, tf), lambda i, j: (layer, 0, j)),
            pl.BlockSpec((None, 1, tf), lambda i, j: (layer, 0, nf + j)),
            pl.BlockSpec((None, tf, d), lambda i, j: (layer, j, 0)),
            gain,
        ],
        out_specs=pl.BlockSpec((tm, d), lambda i, j: (i, 0)),
        out_shape=jax.ShapeDtypeStruct((m, d), F32),
        scratch_shapes=[pltpu.VMEM((tm, d), BF16), pltpu.VMEM((tm, d), F32),
                        pltpu.VMEM((2, nf, CARRY, tf), F32)],
        compiler_params=_params(("arbitrary", "arbitrary"), vmem),
        name="conv_gated_mlp",
    )(h, g1, w_up, w_up, conv_w, conv_w, conv_b, conv_b, w_down, g2)


def _ple_kernel(h_ref, p_ref, wpe_ref, wpg_ref, o_ref):
    h = h_ref[...]
    emb = _dot(p_ref[...].astype(BF16), wpe_ref[...])
    gate = jax.nn.sigmoid(_dot(h.astype(BF16), wpg_ref[...]))
    o_ref[...] = h + emb * gate


def per_layer_embedding(h, p, w_pe, w_pg, layer, *, tm=512):
    m, d = h.shape
    pd = p.shape[-1]
    tm = min(tm, m)
    vmem = (4 * tm * d * 4 + 2 * tm * pd * 4 + 2 * (pd + d) * d * 2 + 3 * tm * d * 4) // MIB + 6
    return pl.pallas_call(
        _ple_kernel,
        grid=(m // tm,),
        in_specs=[
            pl.BlockSpec((tm, d), lambda i: (i, 0)),
            pl.BlockSpec((None, tm, pd), lambda i: (layer, i, 0)),
            pl.BlockSpec((None, pd, d), lambda i: (layer, 0, 0)),
            pl.BlockSpec((None, d, d), lambda i: (layer, 0, 0)),
        ],
        out_specs=pl.BlockSpec((tm, d), lambda i: (i, 0)),
        out_shape=jax.ShapeDtypeStruct((m, d), F32),
        compiler_params=_params(("parallel",), vmem),
        name="per_layer_embedding",
    )(h, p, w_pe, w_pg)


def kernel(x, p, ln_mix_pre, w_in, lb_logits, hgrn_norm, w_out, ln_mix_post, ln_ffn_pre, w_up,
           conv_w, conv_b, w_down, ln_ffn_post, w_pe, w_pg):
    bsz, s, d = x.shape
    depth = w_in.shape[0]
    m = bsz * s
    row = lambda a: a.astype(F32).reshape(depth, 1, a.shape[-1])
    w_in_b, w_out_b, w_up_b, w_down_b = (w.astype(BF16) for w in (w_in, w_out, w_up, w_down))
    w_pe_b, w_pg_b = w_pe.astype(BF16), w_pg.astype(BF16)
    g_mix_pre, g_mix_post, g_ffn_pre, g_ffn_post = map(row, (ln_mix_pre, ln_mix_post, ln_ffn_pre, ln_ffn_post))
    hgrn_w, conv_bias = row(hgrn_norm), row(conv_b)
    lb_all = lower_bounds(lb_logits).reshape(depth, 1, GROUP_WIDTH)
    p2 = p.reshape(depth, m, p.shape[-1])

    h = x.reshape(m, d)
    for l in range(depth):
        n_rec = 4 * GROUP_WIDTH
        rec = norm_matmul(h, g_mix_pre, w_in_b, l, n_rec).reshape(bsz, s, n_rec)
        planes = norm_matmul_planes(h.reshape(bsz, s, d), g_mix_pre, w_in_b, l, n_rec, 3)
        o_rec = hgrn2(rec, lb_all, hgrn_w, l)
        o_att = dilated_attention(planes)
        h = out_proj(h, o_rec.reshape(m, GROUP_WIDTH), o_att.reshape(m, GROUP_WIDTH), w_out_b, g_mix_post, l)
        h = conv_gated_mlp(h, s, g_ffn_pre, w_up_b, conv_w, conv_bias, w_down_b, g_ffn_post, l)
        h = per_layer_embedding(h, p2, w_pe_b, w_pg_b, l)
    return h.reshape(bsz, s, d)
```

```python
import functools

import numpy as np
import jax
import jax.numpy as jnp
from jax import lax
from jax.experimental import pallas as pl
from jax.experimental.pallas import tpu as pltpu

F32 = jnp.float32
BF16 = jnp.bfloat16

EPS = 1e-6
LOG2_E = 1.4426950408889634
HEAD_DIM = 128
N_HEADS = 8
GROUP_WIDTH = N_HEADS * HEAD_DIM
N_STREAMS = 7
CHUNK = 64
ATT_BLOCK = 128
DILATIONS = (1, 4, 16)
CONV_WIDTH = 3
CARRY = 8
MIB = 1024 * 1024
V7X_VMEM_CAP_MIB = 56


def _params(semantics, vmem_mib):
    return pltpu.CompilerParams(
        dimension_semantics=semantics,
        vmem_limit_bytes=min(vmem_mib, V7X_VMEM_CAP_MIB) * MIB,
    )


def _rms(x, gain):
    return x * lax.rsqrt(jnp.mean(x * x, axis=-1, keepdims=True) + EPS) * gain


def _dot(a, b):
    return jnp.dot(a, b, preferred_element_type=F32)


def _dot_nt(a, b):
    return lax.dot_general(a, b, (((1,), (1,)), ((), ())), preferred_element_type=F32)


def _dot_tn(a, b):
    return lax.dot_general(a, b, (((0,), (0,)), ((), ())), preferred_element_type=F32)


def _lb_kernel(logit_ref, o_ref):
    x = logit_ref[...]
    n_layers = x.shape[0]
    e = jnp.exp(x - jnp.max(x, axis=0, keepdims=True))
    sm = e / jnp.sum(e, axis=0, keepdims=True)
    run = sm[0:1]
    rows = [run]
    for l in range(1, n_layers):
        run = run + sm[l:l + 1]
        rows.append(run)
    for l in range(n_layers):
        o_ref[l:l + 1, :] = rows[l] - rows[0]


def lower_bounds(lb_logits):
    return pl.pallas_call(
        _lb_kernel,
        out_shape=jax.ShapeDtypeStruct(lb_logits.shape, F32),
        name="lower_bounds",
    )(lb_logits.astype(F32))


def _norm_matmul_kernel(x_ref, g_ref, w_ref, o_ref, xn_ref):
    @pl.when(pl.program_id(1) == 0)
    def _():
        xn_ref[...] = _rms(x_ref[...], g_ref[...]).astype(BF16)

    o_ref[...] = _dot(xn_ref[...], w_ref[...]).astype(o_ref.dtype)


def norm_matmul(x, gain, w, layer, n_cols, *, tm=1024, tn=1024):
    m, d = x.shape
    tm = min(tm, m)
    vmem = (2 * tm * d * 4 + tm * d * 2 + 2 * d * tn * 2 + 2 * tm * tn * 2 + tm * tn * 4) // MIB + 6
    return pl.pallas_call(
        _norm_matmul_kernel,
        grid=(m // tm, n_cols // tn),
        in_specs=[
            pl.BlockSpec((tm, d), lambda i, j: (i, 0)),
            pl.BlockSpec((None, 1, d), lambda i, j: (layer, 0, 0)),
            pl.BlockSpec((None, d, tn), lambda i, j: (layer, 0, j)),
        ],
        out_specs=pl.BlockSpec((tm, tn), lambda i, j: (i, j)),
        out_shape=jax.ShapeDtypeStruct((m, n_cols), BF16),
        scratch_shapes=[pltpu.VMEM((tm, d), BF16)],
        compiler_params=_params(("parallel", "arbitrary"), vmem),
        name="norm_in_proj",
    )(x, gain, w)


def _norm_matmul_planes_kernel(x_ref, g_ref, w_ref, o1_ref, o4_ref, o16_ref, xn_ref, nat_ref, pl4_ref):
    @pl.when(pl.program_id(2) == 0)
    def _():
        xn_ref[...] = _rms(x_ref[...], g_ref[...]).astype(BF16)

    res = _dot(xn_ref[...], w_ref[...])
    tm, tn = res.shape
    o1_ref[...] = res.astype(BF16)
    n_slabs = tn // HEAD_DIM
    q4, q16 = tm // 4, tm // 16
    for c in range(n_slabs):
        nat_ref[c] = res[:, c * HEAD_DIM:(c + 1) * HEAD_DIM]
    for c in range(n_slabs):
        cols = slice(c * HEAD_DIM, (c + 1) * HEAD_DIM)
        for r4 in range(4):
            plane = nat_ref[c, pl.ds(r4, q4, stride=4), :]
            o4_ref[r4, :, cols] = plane.astype(BF16)
            pl4_ref[c, r4 * q4:(r4 + 1) * q4, :] = plane
    for c in range(n_slabs):
        cols = slice(c * HEAD_DIM, (c + 1) * HEAD_DIM)
        for r4 in range(4):
            for sub in range(4):
                plane = pl4_ref[c, pl.ds(r4 * q4 + sub, q16, stride=4), :]
                o16_ref[4 * sub + r4, :, cols] = plane.astype(BF16)


def norm_matmul_planes(x, gain, w, layer, col0, n_streams, *, tm=512):
    bsz, s, d = x.shape
    tn = GROUP_WIDTH
    tm = min(tm, s)
    assert DILATIONS == (1, 4, 16)
    assert s % tm == 0 and tm % 256 == 0 and col0 % tn == 0
    n_slabs = tn // HEAD_DIM
    vmem = (2 * tm * d * 4 + tm * d * 2 + 2 * d * tn * 2 + 3 * 2 * tm * tn * 2 + 3 * tm * tn * 4) // MIB + 6
    return pl.pallas_call(
        _norm_matmul_planes_kernel,
        grid=(bsz, s // tm, n_streams),
        in_specs=[
            pl.BlockSpec((None, tm, d), lambda b, i, j: (b, i, 0)),
            pl.BlockSpec((None, 1, d), lambda b, i, j: (layer, 0, 0)),
            pl.BlockSpec((None, d, tn), lambda b, i, j: (layer, 0, col0 // tn + j)),
        ],
        out_specs=(
            pl.BlockSpec((None, None, None, tm, tn), lambda b, i, j: (j, b, 0, i, 0)),
            pl.BlockSpec((None, None, 4, tm // 4, tn), lambda b, i, j: (j, b, 0, i, 0)),
            pl.BlockSpec((None, None, 16, tm // 16, tn), lambda b, i, j: (j, b, 0, i, 0)),
        ),
        out_shape=(
            jax.ShapeDtypeStruct((n_streams, bsz, 1, s, tn), BF16),
            jax.ShapeDtypeStruct((n_streams, bsz, 4, s // 4, tn), BF16),
            jax.ShapeDtypeStruct((n_streams, bsz, 16, s // 16, tn), BF16),
        ),
        scratch_shapes=[pltpu.VMEM((tm, d), BF16),
                        pltpu.VMEM((n_slabs, tm, HEAD_DIM), F32),
                        pltpu.VMEM((n_slabs, tm, HEAD_DIM), F32)],
        compiler_params=_params(("parallel", "parallel", "arbitrary"), vmem),
        name="norm_in_proj_planes",
    )(x, gain, w)


def _hgrn_tables():
    c = CHUNK
    t = np.arange(c)
    cum = (t[None, :] <= t[:, None]).astype(np.float32)
    exps = [cum]
    ups, masks = [], []
    half = c // 2
    while half >= 1:
        blk = 2 * half
        mid = (t // blk) * blk + half
        upper = (t % blk) >= half
        u = t[None, :]
        n_up = upper[:, None] & (u >= mid[:, None]) & (u <= t[:, None])
        n_lo = (~upper)[:, None] & (u >= t[:, None] + 1) & (u <= mid[:, None] - 1)
        exps.append((n_up | n_lo).astype(np.float32))
        ups.append(np.broadcast_to(upper[:, None], (c, HEAD_DIM)).astype(np.float32))
        same = (t[:, None] // blk) == (t[None, :] // blk)
        masks.append((upper[:, None] & (~upper)[None, :] & same).astype(np.float32))
        half //= 2
    masks.append(np.eye(c, dtype=np.float32))
    n_all = np.concatenate(exps, axis=0)
    n3 = np.concatenate([n_all, n_all, n_all], axis=1)
    return n3, np.stack(ups), np.stack(masks)


def _split3(x):
    hi = x.astype(BF16)
    r = x - hi.astype(F32)
    mid = r.astype(BF16)
    lo = (r - mid.astype(F32)).astype(BF16)
    return jnp.concatenate([hi, mid, lo], axis=0)


def _hgrn_kernel(q_ref, f_ref, i_ref, g_ref, lb_ref, nw_ref, n3_ref, up_ref, mask_ref,
                 o_ref, state_ref, e_ref):
    c = CHUNK
    n_levels = up_ref.shape[0]

    @pl.when(pl.program_id(1) == 0)
    def _():
        state_ref[...] = jnp.zeros_like(state_ref)

    n_e = n3_ref.shape[0]
    per_iter = e_ref.shape[0] // n_e
    heads = [slice(h * HEAD_DIM, (h + 1) * HEAD_DIM) for h in range(N_HEADS)]

    def body(it, carry):
        rows = pl.ds(pl.multiple_of(it * (per_iter * c), per_iter * c), per_iter * c)
        lb = lb_ref[...]
        q_all, i_all, g_all = q_ref[rows, :], i_ref[rows, :], g_ref[rows, :]
        f_all = lb + (1.0 - lb) * jax.nn.sigmoid(f_ref[rows, :].astype(F32))
        log2_f = jnp.log(f_all) * LOG2_E
        chains = []
        for u in range(per_iter):
            t = slice(u * c, (u + 1) * c)
            e_ref[u * n_e:(u + 1) * n_e, :] = _dot(n3_ref[...], _split3(log2_f[t]))
            for cols in heads:
                chains.append((u, t, cols))
        q = [jax.nn.silu(q_all[t, cols].astype(F32)) for _, t, cols in chains]
        k = [1.0 - f_all[t, cols] for _, t, cols in chains]
        scores = [_dot_nt(q[n].astype(BF16), k[n].astype(BF16)) * mask_ref[n_levels] for n in range(len(chains))]
        for l in range(n_levels):
            half = c >> (l + 1)
            for n, (u, t, cols) in enumerate(chains):
                if half % 8 == 0:
                    qk = jnp.concatenate(
                        [(q[n] if (r // half) % 2 else k[n])[r:r + half] for r in range(0, c, half)], axis=0)
                else:
                    qk = jnp.where(up_ref[l] > 0.5, q[n], k[n])
                e_rows = slice(u * n_e + (l + 1) * c, u * n_e + (l + 2) * c)
                x = (qk * jnp.exp2(e_ref[e_rows, cols])).astype(BF16)
                scores[n] = scores[n] + _dot_nt(x, x) * mask_ref[l]

        outs = []
        for n, (u, t, cols) in enumerate(chains):
            h = n % N_HEADS
            v = i_all[t, cols]
            b = e_ref[u * n_e:u * n_e + c, cols]
            b_last = b[c - 1:c]
            state = state_ref[h]
            o = _dot_nt((q[n] * jnp.exp2(b)).astype(BF16), state.astype(BF16))
            o = o + _dot(scores[n].astype(BF16), v)
            k_dec = (k[n] * jnp.exp2(b_last - b)).astype(BF16)
            state_ref[h] = state * jnp.exp2(b_last) + _dot_tn(v, k_dec)
            outs.append(o)
        for n, (u, t, cols) in enumerate(chains):
            o = outs[n]
            o = o * lax.rsqrt(jnp.mean(o * o, axis=-1, keepdims=True) + EPS)
            o = o * nw_ref[:, cols] * jax.nn.silu(g_all[t, cols].astype(F32))
            outs[n] = o.astype(o_ref.dtype)
        o_ref[rows, :] = jnp.concatenate(
            [jnp.concatenate(outs[u * N_HEADS:(u + 1) * N_HEADS], axis=-1) for u in range(per_iter)], axis=0)
        return carry

    lax.fori_loop(0, q_ref.shape[0] // (per_iter * c), body, 0)


def hgrn2(proj, lb_all, norm_w, layer, *, tt=512, chunks_per_iter=4):
    bsz, s, _ = proj.shape
    tt = min(tt, s)
    assert s % tt == 0 and tt % (chunks_per_iter * CHUNK) == 0
    n3, ups, masks = _hgrn_tables()
    stream = lambda j: pl.BlockSpec((None, tt, GROUP_WIDTH), lambda b, t: (b, t, j))
    vec = pl.BlockSpec((None, 1, GROUP_WIDTH), lambda b, t: (layer, 0, 0))
    const = lambda a: pl.BlockSpec(a.shape, lambda b, t: (0,) * a.ndim)
    return pl.pallas_call(
        _hgrn_kernel,
        grid=(bsz, s // tt),
        in_specs=[stream(0), stream(1), stream(2), stream(3), vec, vec,
                  const(n3), const(ups), const(masks)],
        out_specs=pl.BlockSpec((None, tt, GROUP_WIDTH), lambda b, t: (b, t, 0)),
        out_shape=jax.ShapeDtypeStruct((bsz, s, GROUP_WIDTH), BF16),
        scratch_shapes=[pltpu.VMEM((N_HEADS, HEAD_DIM, HEAD_DIM), F32),
                        pltpu.VMEM((chunks_per_iter * n3.shape[0], GROUP_WIDTH), F32)],
        compiler_params=_params(("parallel", "arbitrary"), 24),
        name="hgrn2",
    )(proj, proj, proj, proj, lb_all, norm_w,
      jnp.asarray(n3, BF16), jnp.asarray(ups, F32), jnp.asarray(masks, F32))


def _attn_kernel(*refs, merge, emit_lse):
    q_ref, kp_ref, kc_ref, vp_ref, vc_ref = refs[:5]
    refs = refs[5:]
    if merge:
        oin_ref, lin_ref = refs[:2]
        refs = refs[2:]
    o_ref = refs[0]
    refs = refs[1:]
    if emit_lse:
        lout_ref = refs[0]
        refs = refs[1:]
    if merge:
        osc_ref, lsc_ref = refs

    blk = ATT_BLOCK
    rows_per_step = q_ref.shape[0]
    n = pl.program_id(2)

    if merge:
        quarter = rows_per_step // 4
        for sub in range(4):
            for h in range(N_HEADS):
                osc_ref[h, pl.ds(sub, quarter, stride=4), :] = (
                    oin_ref[sub, :, h * HEAD_DIM:(h + 1) * HEAD_DIM].astype(F32))
            lsc_ref[pl.ds(sub, quarter, stride=4), :] = lin_ref[sub]

    qi = lax.broadcasted_iota(jnp.int32, (blk, 2 * blk), 0)
    kj = lax.broadcasted_iota(jnp.int32, (blk, 2 * blk), 1)
    band = (kj >= qi) & (kj <= qi + blk)
    bias_inner = jnp.where(band, 0.0, -jnp.inf).astype(F32)
    bias_first = jnp.where(band & ((n > 0) | (kj >= blk)), 0.0, -jnp.inf).astype(F32)
    scale = HEAD_DIM ** -0.5
    lane = lax.broadcasted_iota(jnp.int32, (blk, HEAD_DIM), 1)
    ones = jnp.ones((2 * blk, HEAD_DIM), BF16)

    for j in range(rows_per_step // blk):
        rows = slice(j * blk, (j + 1) * blk)
        bias = bias_first if j == 0 else bias_inner
        lse_tile = jnp.zeros((blk, HEAD_DIM), F32)
        for h in range(N_HEADS):
            cols = slice(h * HEAD_DIM, (h + 1) * HEAD_DIM)
            if j == 0:
                kh = jnp.concatenate([kp_ref[:, cols], kc_ref[0:blk, cols]], axis=0)
                vh = jnp.concatenate([vp_ref[:, cols], vc_ref[0:blk, cols]], axis=0)
            else:
                kh = kc_ref[(j - 1) * blk:(j + 1) * blk, cols]
                vh = vc_ref[(j - 1) * blk:(j + 1) * blk, cols]
            s = _dot_nt(q_ref[rows, cols], kh) + bias
            m = jnp.max(s, axis=-1, keepdims=True)
            p = jnp.exp2((s - m) * (scale * LOG2_E))
            o_den = _dot(p.astype(BF16), jnp.concatenate([vh, ones], axis=1))
            den = o_den[:, HEAD_DIM:]
            o = o_den[:, :HEAD_DIM] / den
            lse = jnp.broadcast_to(m * scale, (blk, HEAD_DIM)) + jnp.log(den)
            if merge:
                lse_prev = jnp.broadcast_to(lsc_ref[rows, h:h + 1], (blk, HEAD_DIM))
                top = jnp.maximum(lse, lse_prev)
                w_prev = jnp.exp(lse_prev - top)
                w_new = jnp.exp(lse - top)
                tot = w_prev + w_new
                o = (osc_ref[h, rows, :] * w_prev + o * w_new) / tot
                lse = top + jnp.log(tot)
            o_ref[rows, cols] = o.astype(o_ref.dtype)
            if emit_lse:
                lse_tile = jnp.where(lane == h, lse, lse_tile)
        if emit_lse:
            lout_ref[rows, :] = lse_tile


def dilated_attention(planes, *, rows_per_step=1024):
    blk = ATT_BLOCK
    o_acc = lse_acc = None
    for bi in reversed(range(len(DILATIONS))):
        d = DILATIONS[bi]
        qkv = planes[bi]
        _, bsz, _, rows, _ = qkv.shape
        step = min(rows_per_step, rows)
        assert rows % step == 0 and step % blk == 0
        merge = o_acc is not None
        last = bi == 0
        cur = lambda col: pl.BlockSpec((None, None, None, step, GROUP_WIDTH),
                                       lambda b, r, n: (col, b, r, n, 0))
        prev = lambda col: pl.BlockSpec(
            (None, None, None, blk, GROUP_WIDTH),
            lambda b, r, n: (col, b, r, jnp.maximum(n * (step // blk) - 1, 0), 0))
        in_specs = [cur(0), prev(1), cur(1), prev(2), cur(2)]
        args = [qkv] * 5
        scratch = []
        if merge:
            assert DILATIONS[bi + 1] == 4 * d
            coarse = lambda width: pl.BlockSpec((None, 4, None, step // 4, width),
                                                lambda b, r, n: (b, 0, r, n, 0))
            in_specs += [coarse(GROUP_WIDTH), coarse(HEAD_DIM)]
            args += [o_acc.reshape(bsz, 4, d, rows // 4, GROUP_WIDTH),
                     lse_acc.reshape(bsz, 4, d, rows // 4, HEAD_DIM)]
            scratch = [pltpu.VMEM((N_HEADS, step, HEAD_DIM), F32), pltpu.VMEM((step, HEAD_DIM), F32)]
        vmem = (12 * step * GROUP_WIDTH * 2 + 3 * step * GROUP_WIDTH * 4) // MIB + 10
        o_spec = pl.BlockSpec((None, None, step, GROUP_WIDTH), lambda b, r, n: (b, r, n, 0))
        l_spec = pl.BlockSpec((None, None, step, HEAD_DIM), lambda b, r, n: (b, r, n, 0))
        o_shape = jax.ShapeDtypeStruct((bsz, d, rows, GROUP_WIDTH), BF16)
        l_shape = jax.ShapeDtypeStruct((bsz, d, rows, HEAD_DIM), F32)
        outs = pl.pallas_call(
            functools.partial(_attn_kernel, merge=merge, emit_lse=not last),
            grid=(bsz, d, rows // step),
            in_specs=in_specs,
            out_specs=o_spec if last else (o_spec, l_spec),
            out_shape=o_shape if last else (o_shape, l_shape),
            scratch_shapes=scratch,
            compiler_params=_params(("parallel", "parallel", "arbitrary"), vmem),
            name=f"dilated_attn_d{d}",
        )(*args)
        o_acc, lse_acc = (outs, None) if last else outs
    bsz, _, s, _ = o_acc.shape
    return o_acc.reshape(bsz, s, GROUP_WIDTH)


def _out_proj_kernel(h_ref, a1_ref, a2_ref, w_ref, g_ref, o_ref):
    half = a1_ref.shape[1]
    y = _dot(a1_ref[...], w_ref[0:half, :]) + _dot(a2_ref[...], w_ref[half:2 * half, :])
    o_ref[...] = h_ref[...] + _rms(y, g_ref[...])


def out_proj(h, a1, a2, w, gain, layer, *, tm=512):
    m, d = h.shape
    ka = a1.shape[1]
    tm = min(tm, m)
    vmem = (4 * tm * d * 4 + 4 * tm * ka * 2 + 2 * 2 * ka * d * 2 + 2 * tm * d * 4) // MIB + 6
    return pl.pallas_call(
        _out_proj_kernel,
        grid=(m // tm,),
        in_specs=[
            pl.BlockSpec((tm, d), lambda i: (i, 0)),
            pl.BlockSpec((tm, ka), lambda i: (i, 0)),
            pl.BlockSpec((tm, ka), lambda i: (i, 0)),
            pl.BlockSpec((None, 2 * ka, d), lambda i: (layer, 0, 0)),
            pl.BlockSpec((None, 1, d), lambda i: (layer, 0, 0)),
        ],
        out_specs=pl.BlockSpec((tm, d), lambda i: (i, 0)),
        out_shape=jax.ShapeDtypeStruct((m, d), F32),
        compiler_params=_params(("parallel",), vmem),
        name="out_proj_norm_res",
    )(h, a1, a2, w, gain)


def _ffn_kernel(h_ref, g1_ref, wg_ref, wv_ref, cwg_ref, cwv_ref, cbg_ref, cbv_ref,
                wd_ref, g2_ref, o_ref, xn_ref, tail_ref, *, tiles_per_seq):
    i, j = pl.program_id(0), pl.program_id(1)
    tm = h_ref.shape[0]

    @pl.when(j == 0)
    def _():
        xn_ref[...] = _rms(h_ref[...], g1_ref[...]).astype(BF16)
        o_ref[...] = jnp.zeros_like(o_ref)

    @pl.when((i == 0) & (j == 0))
    def _():
        tail_ref[...] = jnp.zeros_like(tail_ref)

    xn = xn_ref[...]
    first = (i % tiles_per_seq) == 0

    def conv(w_ref, cw_ref, cb_ref, slot):
        u = _dot(xn, w_ref[...])
        above = jnp.where(first, 0.0, tail_ref[slot, j])
        tail_ref[slot, j] = u[tm - CARRY:tm]
        u = jnp.concatenate([above, u], axis=0)
        cw = cw_ref[...]
        y = cb_ref[...] + cw[2:3] * u + cw[1:2] * pltpu.roll(u, 1, 0) + cw[0:1] * pltpu.roll(u, 2, 0)
        return y[CARRY:CARRY + tm]

    gate = conv(wg_ref, cwg_ref, cbg_ref, 0)
    val = conv(wv_ref, cwv_ref, cbv_ref, 1)
    mid = (jax.nn.gelu(gate, approximate=True) * val).astype(BF16)
    o_ref[...] += _dot(mid, wd_ref[...])

    @pl.when(j == pl.num_programs(1) - 1)
    def _():
        o_ref[...] = h_ref[...] + _rms(o_ref[...], g2_ref[...])


def conv_gated_mlp(h, seq_len, g1, w_up, conv_w, conv_b, w_down, g2, layer, *, tm=1024, tf=512):
    m, d = h.shape
    f = w_down.shape[1]
    tm, tf = min(tm, seq_len), min(tf, f)
    nf = f // tf
    assert seq_len % tm == 0 and f % tf == 0 and tm % CARRY == 0 and CARRY >= CONV_WIDTH - 1
    vmem = (3 * tm * d * 4 + tm * d * 2 + 3 * 2 * d * tf * 2
            + 4 * (tm + CARRY) * tf * 4 + 2 * nf * CARRY * tf * 4) // MIB + 8
    gain = pl.BlockSpec((None, 1, d), lambda i, j: (layer, 0, 0))
    return pl.pallas_call(
        functools.partial(_ffn_kernel, tiles_per_seq=seq_len // tm),
        grid=(m // tm, nf),
        in_specs=[
            pl.BlockSpec((tm, d), lambda i, j: (i, 0), pipeline_mode=pl.Buffered(1)),
            gain,
            pl.BlockSpec((None, d, tf), lambda i, j: (layer, 0, j)),
            pl.BlockSpec((None, d, tf), lambda i, j: (layer, 0, nf + j)),
            pl.BlockSpec((None, CONV_WIDTH, tf), lambda i, j: (layer, 0, j)),
            pl.BlockSpec((None, CONV_WIDTH, tf), lambda i, j: (layer, 0, nf + j)),
            pl.BlockSpec((None, 1, tf), lambda i, j: (layer, 0, j)),
            pl.BlockSpec((None, 1, tf), lambda i, j: (layer, 0, nf + j)),
            pl.BlockSpec((None, tf, d), lambda i, j: (layer, j, 0)),
            gain,
        ],
        out_specs=pl.BlockSpec((tm, d), lambda i, j: (i, 0)),
        out_shape=jax.ShapeDtypeStruct((m, d), F32),
        scratch_shapes=[pltpu.VMEM((tm, d), BF16), pltpu.VMEM((2, nf, CARRY, tf), F32)],
        compiler_params=_params(("arbitrary", "arbitrary"), vmem),
        name="conv_gated_mlp",
    )(h, g1, w_up, w_up, conv_w, conv_w, conv_b, conv_b, w_down, g2)


def _ple_kernel(h_ref, p_ref, wpe_ref, wpg_ref, o_ref):
    h = h_ref[...]
    emb = _dot(p_ref[...].astype(BF16), wpe_ref[...])
    gate = jax.nn.sigmoid(_dot(h.astype(BF16), wpg_ref[...]))
    o_ref[...] = h + emb * gate


def per_layer_embedding(h, p, w_pe, w_pg, layer, *, tm=512):
    m, d = h.shape
    pd = p.shape[-1]
    tm = min(tm, m)
    vmem = (4 * tm * d * 4 + 2 * tm * pd * 4 + 2 * (pd + d) * d * 2 + 3 * tm * d * 4) // MIB + 6
    return pl.pallas_call(
        _ple_kernel,
        grid=(m // tm,),
        in_specs=[
            pl.BlockSpec((tm, d), lambda i: (i, 0)),
            pl.BlockSpec((None, tm, pd), lambda i: (layer, i, 0)),
            pl.BlockSpec((None, pd, d), lambda i: (layer, 0, 0)),
            pl.BlockSpec((None, d, d), lambda i: (layer, 0, 0)),
        ],
        out_specs=pl.BlockSpec((tm, d), lambda i: (i, 0)),
        out_shape=jax.ShapeDtypeStruct((m, d), F32),
        compiler_params=_params(("parallel",), vmem),
        name="per_layer_embedding",
    )(h, p, w_pe, w_pg)


def kernel(x, p, ln_mix_pre, w_in, lb_logits, hgrn_norm, w_out, ln_mix_post, ln_ffn_pre, w_up,
           conv_w, conv_b, w_down, ln_ffn_post, w_pe, w_pg):
    bsz, s, d = x.shape
    depth = w_in.shape[0]
    m = bsz * s
    row = lambda a: a.astype(F32).reshape(depth, 1, a.shape[-1])
    w_in_b, w_out_b, w_up_b, w_down_b = (w.astype(BF16) for w in (w_in, w_out, w_up, w_down))
    w_pe_b, w_pg_b = w_pe.astype(BF16), w_pg.astype(BF16)
    g_mix_pre, g_mix_post, g_ffn_pre, g_ffn_post = map(row, (ln_mix_pre, ln_mix_post, ln_ffn_pre, ln_ffn_post))
    hgrn_w, conv_bias = row(hgrn_norm), row(conv_b)
    lb_all = lower_bounds(lb_logits).reshape(depth, 1, GROUP_WIDTH)
    p2 = p.reshape(depth, m, p.shape[-1])

    h = x.reshape(m, d)
    for l in range(depth):
        n_rec = 4 * GROUP_WIDTH
        rec = norm_matmul(h, g_mix_pre, w_in_b, l, n_rec).reshape(bsz, s, n_rec)
        planes = norm_matmul_planes(h.reshape(bsz, s, d), g_mix_pre, w_in_b, l, n_rec, 3)
        o_rec = hgrn2(rec, lb_all, hgrn_w, l)
        o_att = dilated_attention(planes)
        h = out_proj(h, o_rec.reshape(m, GROUP_WIDTH), o_att.reshape(m, GROUP_WIDTH), w_out_b, g_mix_post, l)
        h = conv_gated_mlp(h, s, g_ffn_pre, w_up_b, conv_w, conv_bias, w_down_b, g_ffn_post, l)
        h = per_layer_embedding(h, p2, w_pe_b, w_pg_b, l)
    return h.reshape(bsz, s, d)
```

```python
import functools

import numpy as np
import jax
import jax.numpy as jnp
from jax import lax
from jax.experimental import pallas as pl
from jax.experimental.pallas import tpu as pltpu

F32 = jnp.float32
BF16 = jnp.bfloat16

EPS = 1e-6
LOG2_E = 1.4426950408889634
HEAD_DIM = 128
N_HEADS = 8
GROUP_WIDTH = N_HEADS * HEAD_DIM
CHUNK = 64
ATT_BLOCK = 128
DILATIONS = (1, 4, 16)
CONV_WIDTH = 3
CARRY = 8
MIB = 1024 * 1024
V7X_VMEM_CAP_MIB = 56


def _params(semantics, vmem_mib):
    return pltpu.CompilerParams(
        dimension_semantics=semantics,
        vmem_limit_bytes=min(vmem_mib, V7X_VMEM_CAP_MIB) * MIB,
    )


def _rms(x, gain):
    return x * lax.rsqrt(jnp.mean(x * x, axis=-1, keepdims=True) + EPS) * gain


def _dot(a, b):
    return jnp.dot(a, b, preferred_element_type=F32)


def _dot_nt(a, b):
    return lax.dot_general(a, b, (((1,), (1,)), ((), ())), preferred_element_type=F32)


def _dot_tn(a, b):
    return lax.dot_general(a, b, (((0,), (0,)), ((), ())), preferred_element_type=F32)


def _lb_kernel(logit_ref, o_ref):
    x = logit_ref[...]
    n_layers = x.shape[0]
    e = jnp.exp(x - jnp.max(x, axis=0, keepdims=True))
    sm = e / jnp.sum(e, axis=0, keepdims=True)
    run = sm[0:1]
    rows = [run]
    for l in range(1, n_layers):
        run = run + sm[l:l + 1]
        rows.append(run)
    for l in range(n_layers):
        o_ref[l:l + 1, :] = rows[l] - rows[0]


def lower_bounds(lb_logits):
    return pl.pallas_call(
        _lb_kernel,
        out_shape=jax.ShapeDtypeStruct(lb_logits.shape, F32),
        name="lower_bounds",
    )(lb_logits.astype(F32))


def _norm_matmul_kernel(x_ref, g_ref, w_ref, o_ref, xn_ref):
    @pl.when(pl.program_id(1) == 0)
    def _():
        xn_ref[...] = _rms(x_ref[...], g_ref[...]).astype(BF16)

    o_ref[...] = _dot(xn_ref[...], w_ref[...]).astype(o_ref.dtype)


def norm_matmul(x, gain, w, layer, n_cols, *, tm=1024, tn=1024):
    m, d = x.shape
    tm = min(tm, m)
    vmem = (2 * tm * d * 4 + tm * d * 2 + 2 * d * tn * 2 + 2 * tm * tn * 2 + tm * tn * 4) // MIB + 6
    return pl.pallas_call(
        _norm_matmul_kernel,
        grid=(m // tm, n_cols // tn),
        in_specs=[
            pl.BlockSpec((tm, d), lambda i, j: (i, 0)),
            pl.BlockSpec((None, 1, d), lambda i, j: (layer, 0, 0)),
            pl.BlockSpec((None, d, tn), lambda i, j: (layer, 0, j)),
        ],
        out_specs=pl.BlockSpec((tm, tn), lambda i, j: (i, j)),
        out_shape=jax.ShapeDtypeStruct((m, n_cols), BF16),
        scratch_shapes=[pltpu.VMEM((tm, d), BF16)],
        compiler_params=_params(("parallel", "arbitrary"), vmem),
        name="norm_in_proj",
    )(x, gain, w)


def _norm_matmul_planes_kernel(x_ref, g_ref, w_ref, o1_ref, o4_ref, o16_ref, xn_ref, nat_ref, pl4_ref):
    @pl.when(pl.program_id(2) == 0)
    def _():
        xn_ref[...] = _rms(x_ref[...], g_ref[...]).astype(BF16)

    res = _dot(xn_ref[...], w_ref[...])
    tm, tn = res.shape
    o1_ref[...] = res.astype(BF16)
    n_slabs = tn // HEAD_DIM
    q4, q16 = tm // 4, tm // 16
    for c in range(n_slabs):
        nat_ref[c] = res[:, c * HEAD_DIM:(c + 1) * HEAD_DIM]
    for r4 in range(4):
        for piece in range(0, q4, q16):
            parts = [nat_ref[c, pl.ds(r4 + 4 * piece, q16, stride=4), :] for c in range(n_slabs)]
            for c in range(n_slabs):
                pl4_ref[c, r4 * q4 + piece:r4 * q4 + piece + q16, :] = parts[c]
            o4_ref[r4, piece:piece + q16, :] = jnp.concatenate(parts, axis=1).astype(BF16)
    for r4 in range(4):
        for sub in range(4):
            rows = jnp.concatenate(
                [pl4_ref[c, pl.ds(r4 * q4 + sub, q16, stride=4), :] for c in range(n_slabs)], axis=1)
            o16_ref[4 * sub + r4] = rows.astype(BF16)


def norm_matmul_planes(x, gain, w, layer, col0, n_streams, *, tm=512):
    bsz, s, d = x.shape
    tn = GROUP_WIDTH
    tm = min(tm, s)
    assert DILATIONS == (1, 4, 16)
    assert s % tm == 0 and tm % 256 == 0 and col0 % tn == 0
    n_slabs = tn // HEAD_DIM
    vmem = (2 * tm * d * 4 + tm * d * 2 + 2 * d * tn * 2 + 3 * 2 * tm * tn * 2 + 3 * tm * tn * 4) // MIB + 6
    return pl.pallas_call(
        _norm_matmul_planes_kernel,
        grid=(bsz, s // tm, n_streams),
        in_specs=[
            pl.BlockSpec((None, tm, d), lambda b, i, j: (b, i, 0)),
            pl.BlockSpec((None, 1, d), lambda b, i, j: (layer, 0, 0)),
            pl.BlockSpec((None, d, tn), lambda b, i, j: (layer, 0, col0 // tn + j)),
        ],
        out_specs=(
            pl.BlockSpec((None, None, None, tm, tn), lambda b, i, j: (j, b, 0, i, 0)),
            pl.BlockSpec((None, None, 4, tm // 4, tn), lambda b, i, j: (j, b, 0, i, 0)),
            pl.BlockSpec((None, None, 16, tm // 16, tn), lambda b, i, j: (j, b, 0, i, 0)),
        ),
        out_shape=(
            jax.ShapeDtypeStruct((n_streams, bsz, 1, s, tn), BF16),
            jax.ShapeDtypeStruct((n_streams, bsz, 4, s // 4, tn), BF16),
            jax.ShapeDtypeStruct((n_streams, bsz, 16, s // 16, tn), BF16),
        ),
        scratch_shapes=[pltpu.VMEM((tm, d), BF16),
                        pltpu.VMEM((n_slabs, tm, HEAD_DIM), F32),
                        pltpu.VMEM((n_slabs, tm, HEAD_DIM), F32)],
        compiler_params=_params(("parallel", "parallel", "arbitrary"), vmem),
        name="norm_in_proj_planes",
    )(x, gain, w)


def _hgrn_tables():
    c = CHUNK
    t = np.arange(c)
    cum = (t[None, :] <= t[:, None]).astype(np.float32)
    exps = [cum]
    ups, masks = [], []
    half = c // 2
    while half >= 1:
        blk = 2 * half
        mid = (t // blk) * blk + half
        upper = (t % blk) >= half
        u = t[None, :]
        n_up = upper[:, None] & (u >= mid[:, None]) & (u <= t[:, None])
        n_lo = (~upper)[:, None] & (u >= t[:, None] + 1) & (u <= mid[:, None] - 1)
        exps.append((n_up | n_lo).astype(np.float32))
        ups.append(np.broadcast_to(upper[:, None], (c, HEAD_DIM)).astype(np.float32))
        same = (t[:, None] // blk) == (t[None, :] // blk)
        masks.append((upper[:, None] & (~upper)[None, :] & same).astype(np.float32))
        half //= 2
    masks.append(np.eye(c, dtype=np.float32))
    n_all = np.concatenate(exps, axis=0)
    n3 = np.concatenate([n_all, n_all, n_all], axis=1)
    return n3, np.stack(ups), np.stack(masks)


def _split3(x):
    hi = x.astype(BF16)
    r = x - hi.astype(F32)
    mid = r.astype(BF16)
    lo = (r - mid.astype(F32)).astype(BF16)
    return jnp.concatenate([hi, mid, lo], axis=0)


def _hgrn_kernel(q_ref, f_ref, i_ref, g_ref, lb_ref, nw_ref, n3_ref, up_ref, mask_ref,
                 o_ref, state_ref, e_ref):
    c = CHUNK
    n_levels = up_ref.shape[0]

    @pl.when(pl.program_id(1) == 0)
    def _():
        state_ref[...] = jnp.zeros_like(state_ref)

    n_e = n3_ref.shape[0]
    per_iter = e_ref.shape[0] // n_e
    heads = [slice(h * HEAD_DIM, (h + 1) * HEAD_DIM) for h in range(N_HEADS)]

    def body(it, carry):
        rows = pl.ds(pl.multiple_of(it * (per_iter * c), per_iter * c), per_iter * c)
        lb = lb_ref[...]
        q_all, i_all, g_all = q_ref[rows, :], i_ref[rows, :], g_ref[rows, :]
        f_all = lb + (1.0 - lb) * jax.nn.sigmoid(f_ref[rows, :].astype(F32))
        log2_f = jnp.log(f_all) * LOG2_E
        chains = []
        for u in range(per_iter):
            t = slice(u * c, (u + 1) * c)
            e_ref[u * n_e:(u + 1) * n_e, :] = _dot(n3_ref[...], _split3(log2_f[t]))
            for cols in heads:
                chains.append((u, t, cols))
        q = [jax.nn.silu(q_all[t, cols].astype(F32)) for _, t, cols in chains]
        k = [1.0 - f_all[t, cols] for _, t, cols in chains]
        scores = [_dot_nt(q[n].astype(BF16), k[n].astype(BF16)) * mask_ref[n_levels] for n in range(len(chains))]
        for l in range(n_levels):
            half = c >> (l + 1)
            for n, (u, t, cols) in enumerate(chains):
                if half % 8 == 0:
                    qk = jnp.concatenate(
                        [(q[n] if (r // half) % 2 else k[n])[r:r + half] for r in range(0, c, half)], axis=0)
                else:
                    qk = jnp.where(up_ref[l] > 0.5, q[n], k[n])
                e_rows = slice(u * n_e + (l + 1) * c, u * n_e + (l + 2) * c)
                x = (qk * jnp.exp2(e_ref[e_rows, cols])).astype(BF16)
                scores[n] = scores[n] + _dot_nt(x, x) * mask_ref[l]

        outs = []
        for n, (u, t, cols) in enumerate(chains):
            h = n % N_HEADS
            v = i_all[t, cols]
            b = e_ref[u * n_e:u * n_e + c, cols]
            b_last = b[c - 1:c]
            state = state_ref[h]
            o = _dot_nt((q[n] * jnp.exp2(b)).astype(BF16), state.astype(BF16))
            o = o + _dot(scores[n].astype(BF16), v)
            k_dec = (k[n] * jnp.exp2(b_last - b)).astype(BF16)
            state_ref[h] = state * jnp.exp2(b_last) + _dot_tn(v, k_dec)
            outs.append(o)
        for n, (u, t, cols) in enumerate(chains):
            o = outs[n]
            o = o * lax.rsqrt(jnp.mean(o * o, axis=-1, keepdims=True) + EPS)
            o = o * nw_ref[:, cols] * jax.nn.silu(g_all[t, cols].astype(F32))
            outs[n] = o.astype(o_ref.dtype)
        o_ref[rows, :] = jnp.concatenate(
            [jnp.concatenate(outs[u * N_HEADS:(u + 1) * N_HEADS], axis=-1) for u in range(per_iter)], axis=0)
        return carry

    lax.fori_loop(0, q_ref.shape[0] // (per_iter * c), body, 0)


def hgrn2(proj, lb_all, norm_w, layer, *, tt=512, chunks_per_iter=4):
    bsz, s, _ = proj.shape
    tt = min(tt, s)
    assert s % tt == 0 and tt % (chunks_per_iter * CHUNK) == 0
    n3, ups, masks = _hgrn_tables()
    stream = lambda j: pl.BlockSpec((None, tt, GROUP_WIDTH), lambda b, t: (b, t, j))
    vec = pl.BlockSpec((None, 1, GROUP_WIDTH), lambda b, t: (layer, 0, 0))
    const = lambda a: pl.BlockSpec(a.shape, lambda b, t: (0,) * a.ndim)
    return pl.pallas_call(
        _hgrn_kernel,
        grid=(bsz, s // tt),
        in_specs=[stream(0), stream(1), stream(2), stream(3), vec, vec,
                  const(n3), const(ups), const(masks)],
        out_specs=pl.BlockSpec((None, tt, GROUP_WIDTH), lambda b, t: (b, t, 0)),
        out_shape=jax.ShapeDtypeStruct((bsz, s, GROUP_WIDTH), BF16),
        scratch_shapes=[pltpu.VMEM((N_HEADS, HEAD_DIM, HEAD_DIM), F32),
                        pltpu.VMEM((chunks_per_iter * n3.shape[0], GROUP_WIDTH), F32)],
        compiler_params=_params(("parallel", "arbitrary"), 24),
        name="hgrn2",
    )(proj, proj, proj, proj, lb_all, norm_w,
      jnp.asarray(n3, BF16), jnp.asarray(ups, F32), jnp.asarray(masks, F32))


def _attn_kernel(*refs, merge, emit_lse):
    q_ref, kp_ref, kc_ref, vp_ref, vc_ref = refs[:5]
    refs = refs[5:]
    if merge:
        oin_ref, lin_ref = refs[:2]
        refs = refs[2:]
    o_ref = refs[0]
    refs = refs[1:]
    if emit_lse:
        lout_ref = refs[0]
        refs = refs[1:]
    if merge:
        osc_ref, lsc_ref = refs

    blk = ATT_BLOCK
    rows_per_step = q_ref.shape[0]
    n = pl.program_id(2)

    if merge:
        quarter = rows_per_step // 4
        for sub in range(4):
            for h in range(N_HEADS):
                osc_ref[h, pl.ds(sub, quarter, stride=4), :] = (
                    oin_ref[sub, :, h * HEAD_DIM:(h + 1) * HEAD_DIM].astype(F32))
            lsc_ref[pl.ds(sub, quarter, stride=4), :] = lin_ref[sub]

    qi = lax.broadcasted_iota(jnp.int32, (blk, 2 * blk), 0)
    kj = lax.broadcasted_iota(jnp.int32, (blk, 2 * blk), 1)
    band = (kj >= qi) & (kj <= qi + blk)
    bias_inner = jnp.where(band, 0.0, -jnp.inf).astype(F32)
    bias_first = jnp.where(band & ((n > 0) | (kj >= blk)), 0.0, -jnp.inf).astype(F32)
    scale = HEAD_DIM ** -0.5
    lane = lax.broadcasted_iota(jnp.int32, (blk, HEAD_DIM), 1)
    ones = jnp.ones((2 * blk, HEAD_DIM), BF16)

    for j in range(rows_per_step // blk):
        rows = slice(j * blk, (j + 1) * blk)
        bias = bias_first if j == 0 else bias_inner
        lse_tile = jnp.zeros((blk, HEAD_DIM), F32)
        for h in range(N_HEADS):
            cols = slice(h * HEAD_DIM, (h + 1) * HEAD_DIM)
            if j == 0:
                kh = jnp.concatenate([kp_ref[:, cols], kc_ref[0:blk, cols]], axis=0)
                vh = jnp.concatenate([vp_ref[:, cols], vc_ref[0:blk, cols]], axis=0)
            else:
                kh = kc_ref[(j - 1) * blk:(j + 1) * blk, cols]
                vh = vc_ref[(j - 1) * blk:(j + 1) * blk, cols]
            s = _dot_nt(q_ref[rows, cols], kh) + bias
            m = jnp.max(s, axis=-1, keepdims=True)
            p = jnp.exp2((s - m) * (scale * LOG2_E))
            o_den = _dot(p.astype(BF16), jnp.concatenate([vh, ones], axis=1))
            den = o_den[:, HEAD_DIM:]
            o = o_den[:, :HEAD_DIM] / den
            lse = jnp.broadcast_to(m * scale, (blk, HEAD_DIM)) + jnp.log(den)
            if merge:
                lse_prev = jnp.broadcast_to(lsc_ref[rows, h:h + 1], (blk, HEAD_DIM))
                top = jnp.maximum(lse, lse_prev)
                w_prev = jnp.exp(lse_prev - top)
                w_new = jnp.exp(lse - top)
                tot = w_prev + w_new
                o = (osc_ref[h, rows, :] * w_prev + o * w_new) / tot
                lse = top + jnp.log(tot)
            o_ref[rows, cols] = o.astype(o_ref.dtype)
            if emit_lse:
                lse_tile = jnp.where(lane == h, lse, lse_tile)
        if emit_lse:
            lout_ref[rows, :] = lse_tile


def dilated_attention(planes, *, rows_per_step=1024):
    blk = ATT_BLOCK
    o_acc = lse_acc = None
    for bi in reversed(range(len(DILATIONS))):
        d = DILATIONS[bi]
        qkv = planes[bi]
        _, bsz, _, rows, _ = qkv.shape
        step = min(rows_per_step, rows)
        assert rows % step == 0 and step % blk == 0
        merge = o_acc is not None
        last = bi == 0
        cur = lambda col: pl.BlockSpec((None, None, None, step, GROUP_WIDTH),
                                       lambda b, r, n: (col, b, r, n, 0))
        prev = lambda col: pl.BlockSpec(
            (None, None, None, blk, GROUP_WIDTH),
            lambda b, r, n: (col, b, r, jnp.maximum(n * (step // blk) - 1, 0), 0))
        in_specs = [cur(0), prev(1), cur(1), prev(2), cur(2)]
        args = [qkv] * 5
        scratch = []
        if merge:
            assert DILATIONS[bi + 1] == 4 * d
            coarse = lambda width: pl.BlockSpec((None, 4, None, step // 4, width),
                                                lambda b, r, n: (b, 0, r, n, 0))
            in_specs += [coarse(GROUP_WIDTH), coarse(HEAD_DIM)]
            args += [o_acc.reshape(bsz, 4, d, rows // 4, GROUP_WIDTH),
                     lse_acc.reshape(bsz, 4, d, rows // 4, HEAD_DIM)]
            scratch = [pltpu.VMEM((N_HEADS, step, HEAD_DIM), F32), pltpu.VMEM((step, HEAD_DIM), F32)]
        vmem = (12 * step * GROUP_WIDTH * 2 + 3 * step * GROUP_WIDTH * 4) // MIB + 10
        o_spec = pl.BlockSpec((None, None, step, GROUP_WIDTH), lambda b, r, n: (b, r, n, 0))
        l_spec = pl.BlockSpec((None, None, step, HEAD_DIM), lambda b, r, n: (b, r, n, 0))
        o_shape = jax.ShapeDtypeStruct((bsz, d, rows, GROUP_WIDTH), BF16)
        l_shape = jax.ShapeDtypeStruct((bsz, d, rows, HEAD_DIM), F32)
        outs = pl.pallas_call(
            functools.partial(_attn_kernel, merge=merge, emit_lse=not last),
            grid=(bsz, d, rows // step),
            in_specs=in_specs,
            out_specs=o_spec if last else (o_spec, l_spec),
            out_shape=o_shape if last else (o_shape, l_shape),
            scratch_shapes=scratch,
            compiler_params=_params(("parallel", "parallel", "arbitrary"), vmem),
            name=f"dilated_attn_d{d}",
        )(*args)
        o_acc, lse_acc = (outs, None) if last else outs
    bsz, _, s, _ = o_acc.shape
    return o_acc.reshape(bsz, s, GROUP_WIDTH)


def _out_proj_kernel(h_ref, a1_ref, a2_ref, w_ref, g_ref, o_ref):
    half = a1_ref.shape[1]
    y = _dot(a1_ref[...], w_ref[0:half, :]) + _dot(a2_ref[...], w_ref[half:2 * half, :])
    o_ref[...] = h_ref[...] + _rms(y, g_ref[...])


def out_proj(h, a1, a2, w, gain, layer, *, tm=512):
    m, d = h.shape
    ka = a1.shape[1]
    tm = min(tm, m)
    vmem = (4 * tm * d * 4 + 4 * tm * ka * 2 + 2 * 2 * ka * d * 2 + 2 * tm * d * 4) // MIB + 6
    return pl.pallas_call(
        _out_proj_kernel,
        grid=(m // tm,),
        in_specs=[
            pl.BlockSpec((tm, d), lambda i: (i, 0)),
            pl.BlockSpec((tm, ka), lambda i: (i, 0)),
            pl.BlockSpec((tm, ka), lambda i: (i, 0)),
            pl.BlockSpec((None, 2 * ka, d), lambda i: (layer, 0, 0)),
            pl.BlockSpec((None, 1, d), lambda i: (layer, 0, 0)),
        ],
        out_specs=pl.BlockSpec((tm, d), lambda i: (i, 0)),
        out_shape=jax.ShapeDtypeStruct((m, d), F32),
        compiler_params=_params(("parallel",), vmem),
        name="out_proj_norm_res",
    )(h, a1, a2, w, gain)


def _ffn_kernel(h_ref, g1_ref, wg_ref, wv_ref, cwg_ref, cwv_ref, cbg_ref, cbv_ref,
                wd_ref, g2_ref, o_ref, xn_ref, tail_ref, *, tiles_per_seq):
    i, j = pl.program_id(0), pl.program_id(1)
    tm = h_ref.shape[0]

    @pl.when(j == 0)
    def _():
        xn_ref[...] = _rms(h_ref[...], g1_ref[...]).astype(BF16)
        o_ref[...] = jnp.zeros_like(o_ref)

    @pl.when((i == 0) & (j == 0))
    def _():
        tail_ref[...] = jnp.zeros_like(tail_ref)

    xn = xn_ref[...]
    first = (i % tiles_per_seq) == 0

    def conv(w_ref, cw_ref, cb_ref, slot):
        u = _dot(xn, w_ref[...])
        above = jnp.where(first, 0.0, tail_ref[slot, j])
        tail_ref[slot, j] = u[tm - CARRY:tm]
        u = jnp.concatenate([above, u], axis=0)
        cw = cw_ref[...]
        y = cb_ref[...] + cw[2:3] * u + cw[1:2] * pltpu.roll(u, 1, 0) + cw[0:1] * pltpu.roll(u, 2, 0)
        return y[CARRY:CARRY + tm]

    gate = conv(wg_ref, cwg_ref, cbg_ref, 0)
    val = conv(wv_ref, cwv_ref, cbv_ref, 1)
    mid = (jax.nn.gelu(gate, approximate=True) * val).astype(BF16)
    o_ref[...] += _dot(mid, wd_ref[...])

    @pl.when(j == pl.num_programs(1) - 1)
    def _():
        o_ref[...] = h_ref[...] + _rms(o_ref[...], g2_ref[...])


def conv_gated_mlp(h, seq_len, g1, w_up, conv_w, conv_b, w_down, g2, layer, *, tm=1024, tf=512):
    m, d = h.shape
    f = w_down.shape[1]
    tm, tf = min(tm, seq_len), min(tf, f)
    nf = f // tf
    assert seq_len % tm == 0 and f % tf == 0 and tm % CARRY == 0 and CARRY >= CONV_WIDTH - 1
    vmem = (3 * tm * d * 4 + tm * d * 2 + 3 * 2 * d * tf * 2
            + 4 * (tm + CARRY) * tf * 4 + 2 * nf * CARRY * tf * 4) // MIB + 8
    gain = pl.BlockSpec((None, 1, d), lambda i, j: (layer, 0, 0))
    return pl.pallas_call(
        functools.partial(_ffn_kernel, tiles_per_seq=seq_len // tm),
        grid=(m // tm, nf),
        in_specs=[
            pl.BlockSpec((tm, d), lambda i, j: (i, 0), pipeline_mode=pl.Buffered(1)),
            gain,
            pl.BlockSpec((None, d, tf), lambda i, j: (layer, 0, j)),
            pl.BlockSpec((None, d, tf), lambda i, j: (layer, 0, nf + j)),
            pl.BlockSpec((None, CONV_WIDTH, tf), lambda i, j: (layer, 0, j)),
            pl.BlockSpec((None, CONV_WIDTH, tf), lambda i, j: (layer, 0, nf + j)),
            pl.BlockSpec((None, 1, tf), lambda i, j: (layer, 0, j)),
            pl.BlockSpec((None, 1, tf), lambda i, j: (layer, 0, nf + j)),
            pl.BlockSpec((None, tf, d), lambda i, j: (layer, j, 0)),
            gain,
        ],
        out_specs=pl.BlockSpec((tm, d), lambda i, j: (i, 0)),
        out_shape=jax.ShapeDtypeStruct((m, d), F32),
        scratch_shapes=[pltpu.VMEM((tm, d), BF16), pltpu.VMEM((2, nf, CARRY, tf), F32)],
        compiler_params=_params(("arbitrary", "arbitrary"), vmem),
        name="conv_gated_mlp",
    )(h, g1, w_up, w_up, conv_w, conv_w, conv_b, conv_b, w_down, g2)


def _ple_kernel(h_ref, p_ref, wpe_ref, wpg_ref, o_ref):
    h = h_ref[...]
    emb = _dot(p_ref[...].astype(BF16), wpe_ref[...])
    gate = jax.nn.sigmoid(_dot(h.astype(BF16), wpg_ref[...]))
    o_ref[...] = h + emb * gate


def per_layer_embedding(h, p, w_pe, w_pg, layer, *, tm=512):
    m, d = h.shape
    pd = p.shape[-1]
    tm = min(tm, m)
    vmem = (4 * tm * d * 4 + 2 * tm * pd * 4 + 2 * (pd + d) * d * 2 + 3 * tm * d * 4) // MIB + 6
    return pl.pallas_call(
        _ple_kernel,
        grid=(m // tm,),
        in_specs=[
            pl.BlockSpec((tm, d), lambda i: (i, 0)),
            pl.BlockSpec((None, tm, pd), lambda i: (layer, i, 0)),
            pl.BlockSpec((None, pd, d), lambda i: (layer, 0, 0)),
            pl.BlockSpec((None, d, d), lambda i: (layer, 0, 0)),
        ],
        out_specs=pl.BlockSpec((tm, d), lambda i: (i, 0)),
        out_shape=jax.ShapeDtypeStruct((m, d), F32),
        compiler_params=_params(("parallel",), vmem),
        name="per_layer_embedding",
    )(h, p, w_pe, w_pg)


def kernel(x, p, ln_mix_pre, w_in, lb_logits, hgrn_norm, w_out, ln_mix_post, ln_ffn_pre, w_up,
           conv_w, conv_b, w_down, ln_ffn_post, w_pe, w_pg):
    bsz, s, d = x.shape
    depth = w_in.shape[0]
    m = bsz * s
    row = lambda a: a.astype(F32).reshape(depth, 1, a.shape[-1])
    w_in_b, w_out_b, w_up_b, w_down_b = (w.astype(BF16) for w in (w_in, w_out, w_up, w_down))
    w_pe_b, w_pg_b = w_pe.astype(BF16), w_pg.astype(BF16)
    g_mix_pre, g_mix_post, g_ffn_pre, g_ffn_post = map(row, (ln_mix_pre, ln_mix_post, ln_ffn_pre, ln_ffn_post))
    hgrn_w, conv_bias = row(hgrn_norm), row(conv_b)
    lb_all = lower_bounds(lb_logits).reshape(depth, 1, GROUP_WIDTH)
    p2 = p.reshape(depth, m, p.shape[-1])

    h = x.reshape(m, d)
    for l in range(depth):
        n_rec = 4 * GROUP_WIDTH
        rec = norm_matmul(h, g_mix_pre, w_in_b, l, n_rec).reshape(bsz, s, n_rec)
        planes = norm_matmul_planes(h.reshape(bsz, s, d), g_mix_pre, w_in_b, l, n_rec, 3)
        o_rec = hgrn2(rec, lb_all, hgrn_w, l)
        o_att = dilated_attention(planes)
        h = out_proj(h, o_rec.reshape(m, GROUP_WIDTH), o_att.reshape(m, GROUP_WIDTH), w_out_b, g_mix_post, l)
        h = conv_gated_mlp(h, s, g_ffn_pre, w_up_b, conv_w, conv_bias, w_down_b, g_ffn_post, l)
        h = per_layer_embedding(h, p2, w_pe_b, w_pg_b, l)
    return h.reshape(bsz, s, d)
```

```python
import functools

import numpy as np
import jax
import jax.numpy as jnp
from jax import lax
from jax.experimental import pallas as pl
from jax.experimental.pallas import tpu as pltpu

F32 = jnp.float32
BF16 = jnp.bfloat16

EPS = 1e-6
LOG2_E = 1.4426950408889634
HEAD_DIM = 128
N_HEADS = 8
GROUP_WIDTH = N_HEADS * HEAD_DIM
N_STREAMS = 7
CHUNK = 64
ATT_BLOCK = 128
DILATIONS = (1, 4, 16)
CONV_WIDTH = 3
CARRY = 8
MIB = 1024 * 1024
V7X_VMEM_CAP_MIB = 56


def _params(semantics, vmem_mib):
    return pltpu.CompilerParams(
        dimension_semantics=semantics,
        vmem_limit_bytes=min(vmem_mib, V7X_VMEM_CAP_MIB) * MIB,
    )


def _rms(x, gain):
    return x * lax.rsqrt(jnp.mean(x * x, axis=-1, keepdims=True) + EPS) * gain


def _dot(a, b):
    return jnp.dot(a, b, preferred_element_type=F32)


def _dot_nt(a, b):
    return lax.dot_general(a, b, (((1,), (1,)), ((), ())), preferred_element_type=F32)


def _dot_tn(a, b):
    return lax.dot_general(a, b, (((0,), (0,)), ((), ())), preferred_element_type=F32)


def _lb_kernel(logit_ref, o_ref):
    x = logit_ref[...]
    n_layers = x.shape[0]
    e = jnp.exp(x - jnp.max(x, axis=0, keepdims=True))
    sm = e / jnp.sum(e, axis=0, keepdims=True)
    run = sm[0:1]
    rows = [run]
    for l in range(1, n_layers):
        run = run + sm[l:l + 1]
        rows.append(run)
    for l in range(n_layers):
        o_ref[l:l + 1, :] = rows[l] - rows[0]


def lower_bounds(lb_logits):
    return pl.pallas_call(
        _lb_kernel,
        out_shape=jax.ShapeDtypeStruct(lb_logits.shape, F32),
        name="lower_bounds",
    )(lb_logits.astype(F32))


def _norm_matmul_kernel(x_ref, g_ref, w_ref, o_ref, xn_ref):
    @pl.when(pl.program_id(1) == 0)
    def _():
        xn_ref[...] = _rms(x_ref[...], g_ref[...]).astype(BF16)

    o_ref[...] = _dot(xn_ref[...], w_ref[...]).astype(o_ref.dtype)


def norm_matmul(x, gain, w, layer, n_cols, *, tm=1024, tn=1024):
    m, d = x.shape
    tm = min(tm, m)
    vmem = (2 * tm * d * 4 + tm * d * 2 + 2 * d * tn * 2 + 2 * tm * tn * 2 + tm * tn * 4) // MIB + 6
    return pl.pallas_call(
        _norm_matmul_kernel,
        grid=(m // tm, n_cols // tn),
        in_specs=[
            pl.BlockSpec((tm, d), lambda i, j: (i, 0)),
            pl.BlockSpec((None, 1, d), lambda i, j: (layer, 0, 0)),
            pl.BlockSpec((None, d, tn), lambda i, j: (layer, 0, j)),
        ],
        out_specs=pl.BlockSpec((tm, tn), lambda i, j: (i, j)),
        out_shape=jax.ShapeDtypeStruct((m, n_cols), BF16),
        scratch_shapes=[pltpu.VMEM((tm, d), BF16)],
        compiler_params=_params(("parallel", "arbitrary"), vmem),
        name="norm_in_proj",
    )(x, gain, w)


def _norm_matmul_planes_kernel(x_ref, g_ref, w_ref, o1_ref, o4_ref, o16_ref, xn_ref, nat_ref, pl4_ref):
    @pl.when(pl.program_id(2) == 0)
    def _():
        xn_ref[...] = _rms(x_ref[...], g_ref[...]).astype(BF16)

    res = _dot(xn_ref[...], w_ref[...])
    tm, tn = res.shape
    o1_ref[...] = res.astype(BF16)
    n_slabs = tn // HEAD_DIM
    q4, q16 = tm // 4, tm // 16
    for c in range(n_slabs):
        nat_ref[c] = res[:, c * HEAD_DIM:(c + 1) * HEAD_DIM]
    for c in range(n_slabs):
        cols = slice(c * HEAD_DIM, (c + 1) * HEAD_DIM)
        for r4 in range(4):
            plane = nat_ref[c, pl.ds(r4, q4, stride=4), :]
            o4_ref[r4, :, cols] = plane.astype(BF16)
            pl4_ref[c, r4 * q4:(r4 + 1) * q4, :] = plane
    for c in range(n_slabs):
        cols = slice(c * HEAD_DIM, (c + 1) * HEAD_DIM)
        for r4 in range(4):
            for sub in range(4):
                plane = pl4_ref[c, pl.ds(r4 * q4 + sub, q16, stride=4), :]
                o16_ref[4 * sub + r4, :, cols] = plane.astype(BF16)


def norm_matmul_planes(x, gain, w, layer, col0, n_streams, *, tm=512):
    bsz, s, d = x.shape
    tn = GROUP_WIDTH
    tm = min(tm, s)
    assert DILATIONS == (1, 4, 16)
    assert s % tm == 0 and tm % 256 == 0 and col0 % tn == 0
    n_slabs = tn // HEAD_DIM
    vmem = (2 * tm * d * 4 + tm * d * 2 + 2 * d * tn * 2 + 3 * 2 * tm * tn * 2 + 3 * tm * tn * 4) // MIB + 6
    return pl.pallas_call(
        _norm_matmul_planes_kernel,
        grid=(bsz, s // tm, n_streams),
        in_specs=[
            pl.BlockSpec((None, tm, d), lambda b, i, j: (b, i, 0)),
            pl.BlockSpec((None, 1, d), lambda b, i, j: (layer, 0, 0)),
            pl.BlockSpec((None, d, tn), lambda b, i, j: (layer, 0, col0 // tn + j)),
        ],
        out_specs=(
            pl.BlockSpec((None, None, None, tm, tn), lambda b, i, j: (j, b, 0, i, 0)),
            pl.BlockSpec((None, None, 4, tm // 4, tn), lambda b, i, j: (j, b, 0, i, 0)),
            pl.BlockSpec((None, None, 16, tm // 16, tn), lambda b, i, j: (j, b, 0, i, 0)),
        ),
        out_shape=(
            jax.ShapeDtypeStruct((n_streams, bsz, 1, s, tn), BF16),
            jax.ShapeDtypeStruct((n_streams, bsz, 4, s // 4, tn), BF16),
            jax.ShapeDtypeStruct((n_streams, bsz, 16, s // 16, tn), BF16),
        ),
        scratch_shapes=[pltpu.VMEM((tm, d), BF16),
                        pltpu.VMEM((n_slabs, tm, HEAD_DIM), F32),
                        pltpu.VMEM((n_slabs, tm, HEAD_DIM), F32)],
        compiler_params=_params(("parallel", "parallel", "arbitrary"), vmem),
        name="norm_in_proj_planes",
    )(x, gain, w)


def _hgrn_tables():
    c = CHUNK
    t = np.arange(c)
    cum = (t[None, :] <= t[:, None]).astype(np.float32)
    exps = [cum]
    ups, masks = [], []
    half = c // 2
    while half >= 1:
        blk = 2 * half
        mid = (t // blk) * blk + half
        upper = (t % blk) >= half
        u = t[None, :]
        n_up = upper[:, None] & (u >= mid[:, None]) & (u <= t[:, None])
        n_lo = (~upper)[:, None] & (u >= t[:, None] + 1) & (u <= mid[:, None] - 1)
        exps.append((n_up | n_lo).astype(np.float32))
        ups.append(np.broadcast_to(upper[:, None], (c, HEAD_DIM)).astype(np.float32))
        same = (t[:, None] // blk) == (t[None, :] // blk)
        masks.append((upper[:, None] & (~upper)[None, :] & same).astype(np.float32))
        half //= 2
    masks.append(np.eye(c, dtype=np.float32))
    n_all = np.concatenate(exps, axis=0)
    n3 = np.concatenate([n_all, n_all, n_all], axis=1)
    return n3, np.stack(ups), np.stack(masks)


def _split3(x):
    hi = x.astype(BF16)
    r = x - hi.astype(F32)
    mid = r.astype(BF16)
    lo = (r - mid.astype(F32)).astype(BF16)
    return jnp.concatenate([hi, mid, lo], axis=0)


def _hgrn_kernel(q_ref, f_ref, i_ref, g_ref, lb_ref, nw_ref, n3_ref, up_ref, mask_ref,
                 o_ref, state_ref, e_ref):
    c = CHUNK
    n_levels = up_ref.shape[0]

    @pl.when(pl.program_id(1) == 0)
    def _():
        state_ref[...] = jnp.zeros_like(state_ref)

    n_e = n3_ref.shape[0]
    per_iter = e_ref.shape[0] // n_e
    heads = [slice(h * HEAD_DIM, (h + 1) * HEAD_DIM) for h in range(N_HEADS)]

    def body(it, carry):
        rows = pl.ds(pl.multiple_of(it * (per_iter * c), per_iter * c), per_iter * c)
        lb = lb_ref[...]
        q_all, i_all, g_all = q_ref[rows, :], i_ref[rows, :], g_ref[rows, :]
        f_all = lb + (1.0 - lb) * jax.nn.sigmoid(f_ref[rows, :].astype(F32))
        log2_f = jnp.log(f_all) * LOG2_E
        chains = []
        for u in range(per_iter):
            t = slice(u * c, (u + 1) * c)
            e_ref[u * n_e:(u + 1) * n_e, :] = _dot(n3_ref[...], _split3(log2_f[t]))
            for cols in heads:
                chains.append((u, t, cols))
        q = [jax.nn.silu(q_all[t, cols].astype(F32)) for _, t, cols in chains]
        k = [1.0 - f_all[t, cols] for _, t, cols in chains]
        scores = [_dot_nt(q[n].astype(BF16), k[n].astype(BF16)) * mask_ref[n_levels] for n in range(len(chains))]
        for l in range(n_levels):
            half = c >> (l + 1)
            for n, (u, t, cols) in enumerate(chains):
                if half % 8 == 0:
                    qk = jnp.concatenate(
                        [(q[n] if (r // half) % 2 else k[n])[r:r + half] for r in range(0, c, half)], axis=0)
                else:
                    qk = jnp.where(up_ref[l] > 0.5, q[n], k[n])
                e_rows = slice(u * n_e + (l + 1) * c, u * n_e + (l + 2) * c)
                x = (qk * jnp.exp2(e_ref[e_rows, cols])).astype(BF16)
                scores[n] = scores[n] + _dot_nt(x, x) * mask_ref[l]

        outs = []
        for n, (u, t, cols) in enumerate(chains):
            h = n % N_HEADS
            v = i_all[t, cols]
            b = e_ref[u * n_e:u * n_e + c, cols]
            b_last = b[c - 1:c]
            state = state_ref[h]
            o = _dot_nt((q[n] * jnp.exp2(b)).astype(BF16), state.astype(BF16))
            o = o + _dot(scores[n].astype(BF16), v)
            k_dec = (k[n] * jnp.exp2(b_last - b)).astype(BF16)
            state_ref[h] = state * jnp.exp2(b_last) + _dot_tn(v, k_dec)
            outs.append(o)
        for n, (u, t, cols) in enumerate(chains):
            o = outs[n]
            o = o * lax.rsqrt(jnp.mean(o * o, axis=-1, keepdims=True) + EPS)
            o = o * nw_ref[:, cols] * jax.nn.silu(g_all[t, cols].astype(F32))
            outs[n] = o.astype(o_ref.dtype)
        o_ref[rows, :] = jnp.concatenate(
            [jnp.concatenate(outs[u * N_HEADS:(u + 1) * N_HEADS], axis=-1) for u in range(per_iter)], axis=0)
        return carry

    lax.fori_loop(0, q_ref.shape[0] // (per_iter * c), body, 0)


def hgrn2(proj, lb_all, norm_w, layer, *, tt=512, chunks_per_iter=4):
    bsz, s, _ = proj.shape
    tt = min(tt, s)
    assert s % tt == 0 and tt % (chunks_per_iter * CHUNK) == 0
    n3, ups, masks = _hgrn_tables()
    stream = lambda j: pl.BlockSpec((None, tt, GROUP_WIDTH), lambda b, t: (b, t, j))
    vec = pl.BlockSpec((None, 1, GROUP_WIDTH), lambda b, t: (layer, 0, 0))
    const = lambda a: pl.BlockSpec(a.shape, lambda b, t: (0,) * a.ndim)
    return pl.pallas_call(
        _hgrn_kernel,
        grid=(bsz, s // tt),
        in_specs=[stream(0), stream(1), stream(2), stream(3), vec, vec,
                  const(n3), const(ups), const(masks)],
        out_specs=pl.BlockSpec((None, tt, GROUP_WIDTH), lambda b, t: (b, t, 0)),
        out_shape=jax.ShapeDtypeStruct((bsz, s, GROUP_WIDTH), BF16),
        scratch_shapes=[pltpu.VMEM((N_HEADS, HEAD_DIM, HEAD_DIM), F32),
                        pltpu.VMEM((chunks_per_iter * n3.shape[0], GROUP_WIDTH), F32)],
        compiler_params=_params(("parallel", "arbitrary"), 24),
        name="hgrn2",
    )(proj, proj, proj, proj, lb_all, norm_w,
      jnp.asarray(n3, BF16), jnp.asarray(ups, F32), jnp.asarray(masks, F32))


def _attn_kernel(*refs, merge, emit_lse):
    q_ref, kp_ref, kc_ref, vp_ref, vc_ref = refs[:5]
    refs = refs[5:]
    if merge:
        oin_ref, lin_ref = refs[:2]
        refs = refs[2:]
    o_ref = refs[0]
    refs = refs[1:]
    if emit_lse:
        lout_ref = refs[0]
        refs = refs[1:]
    if merge:
        osc_ref, lsc_ref = refs

    blk = ATT_BLOCK
    rows_per_step = q_ref.shape[0]
    n = pl.program_id(2)

    if merge:
        quarter = rows_per_step // 4
        for sub in range(4):
            for h in range(N_HEADS):
                osc_ref[h, pl.ds(sub, quarter, stride=4), :] = (
                    oin_ref[sub, :, h * HEAD_DIM:(h + 1) * HEAD_DIM].astype(F32))
            lsc_ref[pl.ds(sub, quarter, stride=4), :] = lin_ref[sub]

    qi = lax.broadcasted_iota(jnp.int32, (blk, 2 * blk), 0)
    kj = lax.broadcasted_iota(jnp.int32, (blk, 2 * blk), 1)
    band = (kj >= qi) & (kj <= qi + blk)
    bias_inner = jnp.where(band, 0.0, -jnp.inf).astype(F32)
    bias_first = jnp.where(band & ((n > 0) | (kj >= blk)), 0.0, -jnp.inf).astype(F32)
    scale = HEAD_DIM ** -0.5
    lane = lax.broadcasted_iota(jnp.int32, (blk, HEAD_DIM), 1)
    ones = jnp.ones((2 * blk, HEAD_DIM), BF16)

    for j in range(rows_per_step // blk):
        rows = slice(j * blk, (j + 1) * blk)
        bias = bias_first if j == 0 else bias_inner
        lse_tile = jnp.zeros((blk, HEAD_DIM), F32)
        for h in range(N_HEADS):
            cols = slice(h * HEAD_DIM, (h + 1) * HEAD_DIM)
            if j == 0:
                kh = jnp.concatenate([kp_ref[:, cols], kc_ref[0:blk, cols]], axis=0)
                vh = jnp.concatenate([vp_ref[:, cols], vc_ref[0:blk, cols]], axis=0)
            else:
                kh = kc_ref[(j - 1) * blk:(j + 1) * blk, cols]
                vh = vc_ref[(j - 1) * blk:(j + 1) * blk, cols]
            s = _dot_nt(q_ref[rows, cols], kh) + bias
            m = jnp.max(s, axis=-1, keepdims=True)
            p = jnp.exp2((s - m) * (scale * LOG2_E))
            o_den = _dot(p.astype(BF16), jnp.concatenate([vh, ones], axis=1))
            den = o_den[:, HEAD_DIM:]
            o = o_den[:, :HEAD_DIM] / den
            lse = jnp.broadcast_to(m * scale, (blk, HEAD_DIM)) + jnp.log(den)
            if merge:
                lse_prev = jnp.broadcast_to(lsc_ref[rows, h:h + 1], (blk, HEAD_DIM))
                top = jnp.maximum(lse, lse_prev)
                w_prev = jnp.exp(lse_prev - top)
                w_new = jnp.exp(lse - top)
                tot = w_prev + w_new
                o = (osc_ref[h, rows, :] * w_prev + o * w_new) / tot
                lse = top + jnp.log(tot)
            o_ref[rows, cols] = o.astype(o_ref.dtype)
            if emit_lse:
                lse_tile = jnp.where(lane == h, lse, lse_tile)
        if emit_lse:
            lout_ref[rows, :] = lse_tile


def dilated_attention(planes, *, rows_per_step=1024):
    blk = ATT_BLOCK
    o_acc = lse_acc = None
    for bi in reversed(range(len(DILATIONS))):
        d = DILATIONS[bi]
        qkv = planes[bi]
        _, bsz, _, rows, _ = qkv.shape
        step = min(rows_per_step, rows)
        assert rows % step == 0 and step % blk == 0
        merge = o_acc is not None
        last = bi == 0
        cur = lambda col: pl.BlockSpec((None, None, None, step, GROUP_WIDTH),
                                       lambda b, r, n: (col, b, r, n, 0))
        prev = lambda col: pl.BlockSpec(
            (None, None, None, blk, GROUP_WIDTH),
            lambda b, r, n: (col, b, r, jnp.maximum(n * (step // blk) - 1, 0), 0))
        in_specs = [cur(0), prev(1), cur(1), prev(2), cur(2)]
        args = [qkv] * 5
        scratch = []
        if merge:
            assert DILATIONS[bi + 1] == 4 * d
            coarse = lambda width: pl.BlockSpec((None, 4, None, step // 4, width),
                                                lambda b, r, n: (b, 0, r, n, 0))
            in_specs += [coarse(GROUP_WIDTH), coarse(HEAD_DIM)]
            args += [o_acc.reshape(bsz, 4, d, rows // 4, GROUP_WIDTH),
                     lse_acc.reshape(bsz, 4, d, rows // 4, HEAD_DIM)]
            scratch = [pltpu.VMEM((N_HEADS, step, HEAD_DIM), F32), pltpu.VMEM((step, HEAD_DIM), F32)]
        vmem = (12 * step * GROUP_WIDTH * 2 + 3 * step * GROUP_WIDTH * 4) // MIB + 10
        o_spec = pl.BlockSpec((None, None, step, GROUP_WIDTH), lambda b, r, n: (b, r, n, 0))
        l_spec = pl.BlockSpec((None, None, step, HEAD_DIM), lambda b, r, n: (b, r, n, 0))
        o_shape = jax.ShapeDtypeStruct((bsz, d, rows, GROUP_WIDTH), BF16)
        l_shape = jax.ShapeDtypeStruct((bsz, d, rows, HEAD_DIM), F32)
        outs = pl.pallas_call(
            functools.partial(_attn_kernel, merge=merge, emit_lse=not last),
            grid=(bsz, d, rows // step),
            in_specs=in_specs,
            out_specs=o_spec if last else (o_spec, l_spec),
            out_shape=o_shape if last else (o_shape, l_shape),
            scratch_shapes=scratch,
            compiler_params=_params(("parallel", "parallel", "arbitrary"), vmem),
            name=f"dilated_attn_d{d}",
        )(*args)
        o_acc, lse_acc = (outs, None) if last else outs
    bsz, _, s, _ = o_acc.shape
    return o_acc.reshape(bsz, s, GROUP_WIDTH)


def _out_proj_kernel(h_ref, a1_ref, a2_ref, w_ref, g_ref, g_next_ref, o_ref, xn_ref):
    half = a1_ref.shape[1]
    y = _dot(a1_ref[...], w_ref[0:half, :]) + _dot(a2_ref[...], w_ref[half:2 * half, :])
    h_new = h_ref[...] + _rms(y, g_ref[...])
    o_ref[...] = h_new
    xn_ref[...] = _rms(h_new, g_next_ref[...]).astype(BF16)


def out_proj(h, a1, a2, w, gain, gain_next, layer, *, tm=512):
    m, d = h.shape
    ka = a1.shape[1]
    tm = min(tm, m)
    vmem = (4 * tm * d * 4 + 4 * tm * ka * 2 + 2 * 2 * ka * d * 2 + 2 * tm * d * 2 + 2 * tm * d * 4) // MIB + 6
    vec = pl.BlockSpec((None, 1, d), lambda i: (layer, 0, 0))
    return pl.pallas_call(
        _out_proj_kernel,
        grid=(m // tm,),
        in_specs=[
            pl.BlockSpec((tm, d), lambda i: (i, 0)),
            pl.BlockSpec((tm, ka), lambda i: (i, 0)),
            pl.BlockSpec((tm, ka), lambda i: (i, 0)),
            pl.BlockSpec((None, 2 * ka, d), lambda i: (layer, 0, 0)),
            vec,
            vec,
        ],
        out_specs=(pl.BlockSpec((tm, d), lambda i: (i, 0)), pl.BlockSpec((tm, d), lambda i: (i, 0))),
        out_shape=(jax.ShapeDtypeStruct((m, d), F32), jax.ShapeDtypeStruct((m, d), BF16)),
        compiler_params=_params(("parallel",), vmem),
        name="out_proj_norm_res",
    )(h, a1, a2, w, gain, gain_next)


def _ffn_kernel(xn_ref, wg_ref, wv_ref, cwg_ref, cwv_ref, cbg_ref, cbv_ref,
                wd_ref, g2_ref, o_ref, tail_ref, *, tiles_per_seq):
    i, j = pl.program_id(0), pl.program_id(1)
    tm = xn_ref.shape[0]

    @pl.when(j == 0)
    def _():
        o_ref[...] = jnp.zeros_like(o_ref)

    @pl.when((i == 0) & (j == 0))
    def _():
        tail_ref[...] = jnp.zeros_like(tail_ref)

    xn = xn_ref[...]
    first = (i % tiles_per_seq) == 0

    def conv(w_ref, cw_ref, cb_ref, slot):
        u = _dot(xn, w_ref[...])
        above = jnp.where(first, 0.0, tail_ref[slot, j])
        tail_ref[slot, j] = u[tm - CARRY:tm]
        u = jnp.concatenate([above, u], axis=0)
        cw = cw_ref[...]
        y = cb_ref[...] + cw[2:3] * u + cw[1:2] * pltpu.roll(u, 1, 0) + cw[0:1] * pltpu.roll(u, 2, 0)
        return y[CARRY:CARRY + tm]

    gate = conv(wg_ref, cwg_ref, cbg_ref, 0)
    val = conv(wv_ref, cwv_ref, cbv_ref, 1)
    mid = (jax.nn.gelu(gate, approximate=True) * val).astype(BF16)
    o_ref[...] += _dot(mid, wd_ref[...])

    @pl.when(j == pl.num_programs(1) - 1)
    def _():
        o_ref[...] = _rms(o_ref[...], g2_ref[...])


def conv_gated_mlp(xn, seq_len, w_up, conv_w, conv_b, w_down, g2, layer, *, tm=1024, tf=512):
    m, d = xn.shape
    f = w_down.shape[1]
    tm, tf = min(tm, seq_len), min(tf, f)
    nf = f // tf
    assert seq_len % tm == 0 and f % tf == 0 and tm % CARRY == 0 and CARRY >= CONV_WIDTH - 1
    vmem = (2 * tm * d * 4 + 2 * tm * d * 2 + 3 * 2 * d * tf * 2
            + 4 * (tm + CARRY) * tf * 4 + 2 * nf * CARRY * tf * 4) // MIB + 8
    gain = pl.BlockSpec((None, 1, d), lambda i, j: (layer, 0, 0))
    return pl.pallas_call(
        functools.partial(_ffn_kernel, tiles_per_seq=seq_len // tm),
        grid=(m // tm, nf),
        in_specs=[
            pl.BlockSpec((tm, d), lambda i, j: (i, 0)),
            pl.BlockSpec((None, d, tf), lambda i, j: (layer, 0, j)),
            pl.BlockSpec((None, d, tf), lambda i, j: (layer, 0, nf + j)),
            pl.BlockSpec((None, CONV_WIDTH, tf), lambda i, j: (layer, 0, j)),
            pl.BlockSpec((None, CONV_WIDTH, tf), lambda i, j: (layer, 0, nf + j)),
            pl.BlockSpec((None, 1, tf), lambda i, j: (layer, 0, j)),
            pl.BlockSpec((None, 1, tf), lambda i, j: (layer, 0, nf + j)),
            pl.BlockSpec((None, tf, d), lambda i, j: (layer, j, 0)),
            gain,
        ],
        out_specs=pl.BlockSpec((tm, d), lambda i, j: (i, 0)),
        out_shape=jax.ShapeDtypeStruct((m, d), F32),
        scratch_shapes=[pltpu.VMEM((2, nf, CARRY, tf), F32)],
        compiler_params=_params(("arbitrary", "arbitrary"), vmem),
        name="conv_gated_mlp",
    )(xn, w_up, w_up, conv_w, conv_w, conv_b, conv_b, w_down, g2)


def _ple_kernel(h_ref, y_ref, p_ref, wpe_ref, wpg_ref, o_ref):
    h = h_ref[...] + y_ref[...]
    emb = _dot(p_ref[...].astype(BF16), wpe_ref[...])
    gate = jax.nn.sigmoid(_dot(h.astype(BF16), wpg_ref[...]))
    o_ref[...] = h + emb * gate


def per_layer_embedding(h, y, p, w_pe, w_pg, layer, *, tm=512):
    m, d = h.shape
    pd = p.shape[-1]
    tm = min(tm, m)
    vmem = (6 * tm * d * 4 + 2 * tm * pd * 4 + 2 * (pd + d) * d * 2 + 3 * tm * d * 4) // MIB + 6
    return pl.pallas_call(
        _ple_kernel,
        grid=(m // tm,),
        in_specs=[
            pl.BlockSpec((tm, d), lambda i: (i, 0)),
            pl.BlockSpec((tm, d), lambda i: (i, 0)),
            pl.BlockSpec((None, tm, pd), lambda i: (layer, i, 0)),
            pl.BlockSpec((None, pd, d), lambda i: (layer, 0, 0)),
            pl.BlockSpec((None, d, d), lambda i: (layer, 0, 0)),
        ],
        out_specs=pl.BlockSpec((tm, d), lambda i: (i, 0)),
        out_shape=jax.ShapeDtypeStruct((m, d), F32),
        compiler_params=_params(("parallel",), vmem),
        name="per_layer_embedding",
    )(h, y, p, w_pe, w_pg)


def kernel(x, p, ln_mix_pre, w_in, lb_logits, hgrn_norm, w_out, ln_mix_post, ln_ffn_pre, w_up,
           conv_w, conv_b, w_down, ln_ffn_post, w_pe, w_pg):
    bsz, s, d = x.shape
    depth = w_in.shape[0]
    m = bsz * s
    row = lambda a: a.astype(F32).reshape(depth, 1, a.shape[-1])
    w_in_b, w_out_b, w_up_b, w_down_b = (w.astype(BF16) for w in (w_in, w_out, w_up, w_down))
    w_pe_b, w_pg_b = w_pe.astype(BF16), w_pg.astype(BF16)
    g_mix_pre, g_mix_post, g_ffn_pre, g_ffn_post = map(row, (ln_mix_pre, ln_mix_post, ln_ffn_pre, ln_ffn_post))
    hgrn_w, conv_bias = row(hgrn_norm), row(conv_b)
    lb_all = lower_bounds(lb_logits).reshape(depth, 1, GROUP_WIDTH)
    p2 = p.reshape(depth, m, p.shape[-1])

    h = x.reshape(m, d)
    for l in range(depth):
        n_rec = 4 * GROUP_WIDTH
        rec = norm_matmul(h, g_mix_pre, w_in_b, l, n_rec).reshape(bsz, s, n_rec)
        planes = norm_matmul_planes(h.reshape(bsz, s, d), g_mix_pre, w_in_b, l, n_rec, 3)
        o_rec = hgrn2(rec, lb_all, hgrn_w, l)
        o_att = dilated_attention(planes)
        h, xn = out_proj(h, o_rec.reshape(m, GROUP_WIDTH), o_att.reshape(m, GROUP_WIDTH), w_out_b,
                         g_mix_post, g_ffn_pre, l)
        y = conv_gated_mlp(xn, s, w_up_b, conv_w, conv_bias, w_down_b, g_ffn_post, l)
        h = per_layer_embedding(h, y, p2, w_pe_b, w_pg_b, l)
    return h.reshape(bsz, s, d)
```

```python
import functools

import numpy as np
import jax
import jax.numpy as jnp
from jax import lax
from jax.experimental import pallas as pl
from jax.experimental.pallas import tpu as pltpu

F32 = jnp.float32
BF16 = jnp.bfloat16

EPS = 1e-6
LOG2_E = 1.4426950408889634
HEAD_DIM = 128
N_HEADS = 8
GROUP_WIDTH = N_HEADS * HEAD_DIM
N_STREAMS = 7
CHUNK = 64
ATT_BLOCK = 128
DILATIONS = (1, 4, 16)
CONV_WIDTH = 3
CARRY = 8
MIB = 1024 * 1024
V7X_VMEM_CAP_MIB = 56


def _params(semantics, vmem_mib):
    return pltpu.CompilerParams(
        dimension_semantics=semantics,
        vmem_limit_bytes=min(vmem_mib, V7X_VMEM_CAP_MIB) * MIB,
    )


def _rms(x, gain):
    return x * lax.rsqrt(jnp.mean(x * x, axis=-1, keepdims=True) + EPS) * gain


def _dot(a, b):
    return jnp.dot(a, b, preferred_element_type=F32)


def _dot_nt(a, b):
    return lax.dot_general(a, b, (((1,), (1,)), ((), ())), preferred_element_type=F32)


def _dot_tn(a, b):
    return lax.dot_general(a, b, (((0,), (0,)), ((), ())), preferred_element_type=F32)


def _lb_kernel(logit_ref, o_ref):
    x = logit_ref[...]
    n_layers = x.shape[0]
    e = jnp.exp(x - jnp.max(x, axis=0, keepdims=True))
    sm = e / jnp.sum(e, axis=0, keepdims=True)
    run = sm[0:1]
    rows = [run]
    for l in range(1, n_layers):
        run = run + sm[l:l + 1]
        rows.append(run)
    for l in range(n_layers):
        o_ref[l:l + 1, :] = rows[l] - rows[0]


def lower_bounds(lb_logits):
    return pl.pallas_call(
        _lb_kernel,
        out_shape=jax.ShapeDtypeStruct(lb_logits.shape, F32),
        name="lower_bounds",
    )(lb_logits.astype(F32))


def _norm_matmul_kernel(x_ref, g_ref, w_ref, o_ref, xn_ref):
    @pl.when(pl.program_id(1) == 0)
    def _():
        xn_ref[...] = _rms(x_ref[...], g_ref[...]).astype(BF16)

    o_ref[...] = _dot(xn_ref[...], w_ref[...]).astype(o_ref.dtype)


def norm_matmul(x, gain, w, layer, n_cols, *, tm=1024, tn=1024):
    m, d = x.shape
    tm = min(tm, m)
    vmem = (2 * tm * d * 4 + 2 * tm * d * 2 + 2 * d * tn * 2 + 2 * tm * tn * 2 + tm * tn * 4) // MIB + 6
    return pl.pallas_call(
        _norm_matmul_kernel,
        grid=(m // tm, n_cols // tn),
        in_specs=[
            pl.BlockSpec((tm, d), lambda i, j: (i, 0)),
            pl.BlockSpec((None, 1, d), lambda i, j: (layer, 0, 0)),
            pl.BlockSpec((None, d, tn), lambda i, j: (layer, 0, j)),
        ],
        out_specs=(pl.BlockSpec((tm, tn), lambda i, j: (i, j)), pl.BlockSpec((tm, d), lambda i, j: (i, 0))),
        out_shape=(jax.ShapeDtypeStruct((m, n_cols), BF16), jax.ShapeDtypeStruct((m, d), BF16)),
        compiler_params=_params(("parallel", "arbitrary"), vmem),
        name="norm_in_proj",
    )(x, gain, w)


def _matmul_planes_kernel(xn_ref, w_ref, o1_ref, o4_ref, o16_ref, nat_ref, pl4_ref):
    res = _dot(xn_ref[...], w_ref[...])
    tm, tn = res.shape
    o1_ref[...] = res.astype(BF16)
    n_slabs = tn // HEAD_DIM
    q4, q16 = tm // 4, tm // 16
    for c in range(n_slabs):
        nat_ref[c] = res[:, c * HEAD_DIM:(c + 1) * HEAD_DIM]
    for c in range(n_slabs):
        cols = slice(c * HEAD_DIM, (c + 1) * HEAD_DIM)
        for r4 in range(4):
            plane = nat_ref[c, pl.ds(r4, q4, stride=4), :]
            o4_ref[r4, :, cols] = plane.astype(BF16)
            pl4_ref[c, r4 * q4:(r4 + 1) * q4, :] = plane
    for c in range(n_slabs):
        cols = slice(c * HEAD_DIM, (c + 1) * HEAD_DIM)
        for r4 in range(4):
            for sub in range(4):
                plane = pl4_ref[c, pl.ds(r4 * q4 + sub, q16, stride=4), :]
                o16_ref[4 * sub + r4, :, cols] = plane.astype(BF16)


def matmul_planes(xn, w, layer, col0, n_streams, *, tm=512):
    bsz, s, d = xn.shape
    tn = GROUP_WIDTH
    tm = min(tm, s)
    assert DILATIONS == (1, 4, 16)
    assert s % tm == 0 and tm % 256 == 0 and col0 % tn == 0
    n_slabs = tn // HEAD_DIM
    vmem = (2 * tm * d * 2 + 2 * d * tn * 2 + 3 * 2 * tm * tn * 2 + 3 * tm * tn * 4) // MIB + 6
    return pl.pallas_call(
        _matmul_planes_kernel,
        grid=(bsz, s // tm, n_streams),
        in_specs=[
            pl.BlockSpec((None, tm, d), lambda b, i, j: (b, i, 0)),
            pl.BlockSpec((None, d, tn), lambda b, i, j: (layer, 0, col0 // tn + j)),
        ],
        out_specs=(
            pl.BlockSpec((None, None, None, tm, tn), lambda b, i, j: (j, b, 0, i, 0)),
            pl.BlockSpec((None, None, 4, tm // 4, tn), lambda b, i, j: (j, b, 0, i, 0)),
            pl.BlockSpec((None, None, 16, tm // 16, tn), lambda b, i, j: (j, b, 0, i, 0)),
        ),
        out_shape=(
            jax.ShapeDtypeStruct((n_streams, bsz, 1, s, tn), BF16),
            jax.ShapeDtypeStruct((n_streams, bsz, 4, s // 4, tn), BF16),
            jax.ShapeDtypeStruct((n_streams, bsz, 16, s // 16, tn), BF16),
        ),
        scratch_shapes=[pltpu.VMEM((n_slabs, tm, HEAD_DIM), F32),
                        pltpu.VMEM((n_slabs, tm, HEAD_DIM), F32)],
        compiler_params=_params(("parallel", "parallel", "arbitrary"), vmem),
        name="in_proj_planes",
    )(xn, w)


def _hgrn_tables():
    c = CHUNK
    t = np.arange(c)
    cum = (t[None, :] <= t[:, None]).astype(np.float32)
    exps = [cum]
    ups, masks = [], []
    half = c // 2
    while half >= 1:
        blk = 2 * half
        mid = (t // blk) * blk + half
        upper = (t % blk) >= half
        u = t[None, :]
        n_up = upper[:, None] & (u >= mid[:, None]) & (u <= t[:, None])
        n_lo = (~upper)[:, None] & (u >= t[:, None] + 1) & (u <= mid[:, None] - 1)
        exps.append((n_up | n_lo).astype(np.float32))
        ups.append(np.broadcast_to(upper[:, None], (c, HEAD_DIM)).astype(np.float32))
        same = (t[:, None] // blk) == (t[None, :] // blk)
        masks.append((upper[:, None] & (~upper)[None, :] & same).astype(np.float32))
        half //= 2
    masks.append(np.eye(c, dtype=np.float32))
    n_all = np.concatenate(exps, axis=0)
    n3 = np.concatenate([n_all, n_all, n_all], axis=1)
    return n3, np.stack(ups), np.stack(masks)


def _split3(x):
    hi = x.astype(BF16)
    r = x - hi.astype(F32)
    mid = r.astype(BF16)
    lo = (r - mid.astype(F32)).astype(BF16)
    return jnp.concatenate([hi, mid, lo], axis=0)


def _hgrn_kernel(q_ref, f_ref, i_ref, g_ref, lb_ref, nw_ref, n3_ref, up_ref, mask_ref,
                 o_ref, state_ref, e_ref):
    c = CHUNK
    n_levels = up_ref.shape[0]

    @pl.when(pl.program_id(1) == 0)
    def _():
        state_ref[...] = jnp.zeros_like(state_ref)

    n_e = n3_ref.shape[0]
    per_iter = e_ref.shape[0] // n_e
    heads = [slice(h * HEAD_DIM, (h + 1) * HEAD_DIM) for h in range(N_HEADS)]

    def body(it, carry):
        rows = pl.ds(pl.multiple_of(it * (per_iter * c), per_iter * c), per_iter * c)
        lb = lb_ref[...]
        q_all, i_all, g_all = q_ref[rows, :], i_ref[rows, :], g_ref[rows, :]
        f_all = lb + (1.0 - lb) * jax.nn.sigmoid(f_ref[rows, :].astype(F32))
        log2_f = jnp.log(f_all) * LOG2_E
        chains = []
        for u in range(per_iter):
            t = slice(u * c, (u + 1) * c)
            e_ref[u * n_e:(u + 1) * n_e, :] = _dot(n3_ref[...], _split3(log2_f[t]))
            for cols in heads:
                chains.append((u, t, cols))
        q = [jax.nn.silu(q_all[t, cols].astype(F32)) for _, t, cols in chains]
        k = [1.0 - f_all[t, cols] for _, t, cols in chains]
        scores = [_dot_nt(q[n].astype(BF16), k[n].astype(BF16)) * mask_ref[n_levels] for n in range(len(chains))]
        for l in range(n_levels):
            half = c >> (l + 1)
            for n, (u, t, cols) in enumerate(chains):
                if half % 8 == 0:
                    qk = jnp.concatenate(
                        [(q[n] if (r // half) % 2 else k[n])[r:r + half] for r in range(0, c, half)], axis=0)
                else:
                    qk = jnp.where(up_ref[l] > 0.5, q[n], k[n])
                e_rows = slice(u * n_e + (l + 1) * c, u * n_e + (l + 2) * c)
                x = (qk * jnp.exp2(e_ref[e_rows, cols])).astype(BF16)
                scores[n] = scores[n] + _dot_nt(x, x) * mask_ref[l]

        outs = []
        for n, (u, t, cols) in enumerate(chains):
            h = n % N_HEADS
            v = i_all[t, cols]
            b = e_ref[u * n_e:u * n_e + c, cols]
            b_last = b[c - 1:c]
            state = state_ref[h]
            o = _dot_nt((q[n] * jnp.exp2(b)).astype(BF16), state.astype(BF16))
            o = o + _dot(scores[n].astype(BF16), v)
            k_dec = (k[n] * jnp.exp2(b_last - b)).astype(BF16)
            state_ref[h] = state * jnp.exp2(b_last) + _dot_tn(v, k_dec)
            outs.append(o)
        for n, (u, t, cols) in enumerate(chains):
            o = outs[n]
            o = o * lax.rsqrt(jnp.mean(o * o, axis=-1, keepdims=True) + EPS)
            o = o * nw_ref[:, cols] * jax.nn.silu(g_all[t, cols].astype(F32))
            outs[n] = o.astype(o_ref.dtype)
        o_ref[rows, :] = jnp.concatenate(
            [jnp.concatenate(outs[u * N_HEADS:(u + 1) * N_HEADS], axis=-1) for u in range(per_iter)], axis=0)
        return carry

    lax.fori_loop(0, q_ref.shape[0] // (per_iter * c), body, 0)


def hgrn2(proj, lb_all, norm_w, layer, *, tt=512, chunks_per_iter=4):
    bsz, s, _ = proj.shape
    tt = min(tt, s)
    assert s % tt == 0 and tt % (chunks_per_iter * CHUNK) == 0
    n3, ups, masks = _hgrn_tables()
    stream = lambda j: pl.BlockSpec((None, tt, GROUP_WIDTH), lambda b, t: (b, t, j))
    vec = pl.BlockSpec((None, 1, GROUP_WIDTH), lambda b, t: (layer, 0, 0))
    const = lambda a: pl.BlockSpec(a.shape, lambda b, t: (0,) * a.ndim)
    return pl.pallas_call(
        _hgrn_kernel,
        grid=(bsz, s // tt),
        in_specs=[stream(0), stream(1), stream(2), stream(3), vec, vec,
                  const(n3), const(ups), const(masks)],
        out_specs=pl.BlockSpec((None, tt, GROUP_WIDTH), lambda b, t: (b, t, 0)),
        out_shape=jax.ShapeDtypeStruct((bsz, s, GROUP_WIDTH), BF16),
        scratch_shapes=[pltpu.VMEM((N_HEADS, HEAD_DIM, HEAD_DIM), F32),
                        pltpu.VMEM((chunks_per_iter * n3.shape[0], GROUP_WIDTH), F32)],
        compiler_params=_params(("parallel", "arbitrary"), 24),
        name="hgrn2",
    )(proj, proj, proj, proj, lb_all, norm_w,
      jnp.asarray(n3, BF16), jnp.asarray(ups, F32), jnp.asarray(masks, F32))


def _attn_kernel(*refs, merge, emit_lse):
    q_ref, kp_ref, kc_ref, vp_ref, vc_ref = refs[:5]
    refs = refs[5:]
    if merge:
        oin_ref, lin_ref = refs[:2]
        refs = refs[2:]
    o_ref = refs[0]
    refs = refs[1:]
    if emit_lse:
        lout_ref = refs[0]
        refs = refs[1:]
    if merge:
        osc_ref, lsc_ref = refs

    blk = ATT_BLOCK
    rows_per_step = q_ref.shape[0]
    n = pl.program_id(2)

    if merge:
        quarter = rows_per_step // 4
        for sub in range(4):
            for h in range(N_HEADS):
                osc_ref[h, pl.ds(sub, quarter, stride=4), :] = (
                    oin_ref[sub, :, h * HEAD_DIM:(h + 1) * HEAD_DIM].astype(F32))
            lsc_ref[pl.ds(sub, quarter, stride=4), :] = lin_ref[sub]

    qi = lax.broadcasted_iota(jnp.int32, (blk, 2 * blk), 0)
    kj = lax.broadcasted_iota(jnp.int32, (blk, 2 * blk), 1)
    band = (kj >= qi) & (kj <= qi + blk)
    bias_inner = jnp.where(band, 0.0, -jnp.inf).astype(F32)
    bias_first = jnp.where(band & ((n > 0) | (kj >= blk)), 0.0, -jnp.inf).astype(F32)
    scale = HEAD_DIM ** -0.5
    lane = lax.broadcasted_iota(jnp.int32, (blk, HEAD_DIM), 1)
    ones = jnp.ones((2 * blk, HEAD_DIM), BF16)

    for j in range(rows_per_step // blk):
        rows = slice(j * blk, (j + 1) * blk)
        bias = bias_first if j == 0 else bias_inner
        lse_tile = jnp.zeros((blk, HEAD_DIM), F32)
        for h in range(N_HEADS):
            cols = slice(h * HEAD_DIM, (h + 1) * HEAD_DIM)
            if j == 0:
                kh = jnp.concatenate([kp_ref[:, cols], kc_ref[0:blk, cols]], axis=0)
                vh = jnp.concatenate([vp_ref[:, cols], vc_ref[0:blk, cols]], axis=0)
            else:
                kh = kc_ref[(j - 1) * blk:(j + 1) * blk, cols]
                vh = vc_ref[(j - 1) * blk:(j + 1) * blk, cols]
            s = _dot_nt(q_ref[rows, cols], kh) + bias
            m = jnp.max(s, axis=-1, keepdims=True)
            p = jnp.exp2((s - m) * (scale * LOG2_E))
            o_den = _dot(p.astype(BF16), jnp.concatenate([vh, ones], axis=1))
            den = o_den[:, HEAD_DIM:]
            o = o_den[:, :HEAD_DIM] / den
            lse = jnp.broadcast_to(m * scale, (blk, HEAD_DIM)) + jnp.log(den)
            if merge:
                lse_prev = jnp.broadcast_to(lsc_ref[rows, h:h + 1], (blk, HEAD_DIM))
                top = jnp.maximum(lse, lse_prev)
                w_prev = jnp.exp(lse_prev - top)
                w_new = jnp.exp(lse - top)
                tot = w_prev + w_new
                o = (osc_ref[h, rows, :] * w_prev + o * w_new) / tot
                lse = top + jnp.log(tot)
            o_ref[rows, cols] = o.astype(o_ref.dtype)
            if emit_lse:
                lse_tile = jnp.where(lane == h, lse, lse_tile)
        if emit_lse:
            lout_ref[rows, :] = lse_tile


def dilated_attention(planes, *, rows_per_step=1024):
    blk = ATT_BLOCK
    o_acc = lse_acc = None
    for bi in reversed(range(len(DILATIONS))):
        d = DILATIONS[bi]
        qkv = planes[bi]
        _, bsz, _, rows, _ = qkv.shape
        step = min(rows_per_step, rows)
        assert rows % step == 0 and step % blk == 0
        merge = o_acc is not None
        last = bi == 0
        cur = lambda col: pl.BlockSpec((None, None, None, step, GROUP_WIDTH),
                                       lambda b, r, n: (col, b, r, n, 0))
        prev = lambda col: pl.BlockSpec(
            (None, None, None, blk, GROUP_WIDTH),
            lambda b, r, n: (col, b, r, jnp.maximum(n * (step // blk) - 1, 0), 0))
        in_specs = [cur(0), prev(1), cur(1), prev(2), cur(2)]
        args = [qkv] * 5
        scratch = []
        if merge:
            assert DILATIONS[bi + 1] == 4 * d
            coarse = lambda width: pl.BlockSpec((None, 4, None, step // 4, width),
                                                lambda b, r, n: (b, 0, r, n, 0))
            in_specs += [coarse(GROUP_WIDTH), coarse(HEAD_DIM)]
            args += [o_acc.reshape(bsz, 4, d, rows // 4, GROUP_WIDTH),
                     lse_acc.reshape(bsz, 4, d, rows // 4, HEAD_DIM)]
            scratch = [pltpu.VMEM((N_HEADS, step, HEAD_DIM), F32), pltpu.VMEM((step, HEAD_DIM), F32)]
        vmem = (12 * step * GROUP_WIDTH * 2 + 3 * step * GROUP_WIDTH * 4) // MIB + 10
        o_spec = pl.BlockSpec((None, None, step, GROUP_WIDTH), lambda b, r, n: (b, r, n, 0))
        l_spec = pl.BlockSpec((None, None, step, HEAD_DIM), lambda b, r, n: (b, r, n, 0))
        o_shape = jax.ShapeDtypeStruct((bsz, d, rows, GROUP_WIDTH), BF16)
        l_shape = jax.ShapeDtypeStruct((bsz, d, rows, HEAD_DIM), F32)
        outs = pl.pallas_call(
            functools.partial(_attn_kernel, merge=merge, emit_lse=not last),
            grid=(bsz, d, rows // step),
            in_specs=in_specs,
            out_specs=o_spec if last else (o_spec, l_spec),
            out_shape=o_shape if last else (o_shape, l_shape),
            scratch_shapes=scratch,
            compiler_params=_params(("parallel", "parallel", "arbitrary"), vmem),
            name=f"dilated_attn_d{d}",
        )(*args)
        o_acc, lse_acc = (outs, None) if last else outs
    bsz, _, s, _ = o_acc.shape
    return o_acc.reshape(bsz, s, GROUP_WIDTH)


def _out_proj_kernel(h_ref, a1_ref, a2_ref, w_ref, g_ref, g_next_ref, o_ref, xn_ref):
    half = a1_ref.shape[1]
    y = _dot(a1_ref[...], w_ref[0:half, :]) + _dot(a2_ref[...], w_ref[half:2 * half, :])
    h_new = h_ref[...] + _rms(y, g_ref[...])
    o_ref[...] = h_new
    xn_ref[...] = _rms(h_new, g_next_ref[...]).astype(BF16)


def out_proj(h, a1, a2, w, gain, gain_next, layer, *, tm=512):
    m, d = h.shape
    ka = a1.shape[1]
    tm = min(tm, m)
    vmem = (4 * tm * d * 4 + 4 * tm * ka * 2 + 2 * 2 * ka * d * 2 + 2 * tm * d * 2 + 2 * tm * d * 4) // MIB + 6
    vec = pl.BlockSpec((None, 1, d), lambda i: (layer, 0, 0))
    return pl.pallas_call(
        _out_proj_kernel,
        grid=(m // tm,),
        in_specs=[
            pl.BlockSpec((tm, d), lambda i: (i, 0)),
            pl.BlockSpec((tm, ka), lambda i: (i, 0)),
            pl.BlockSpec((tm, ka), lambda i: (i, 0)),
            pl.BlockSpec((None, 2 * ka, d), lambda i: (layer, 0, 0)),
            vec,
            vec,
        ],
        out_specs=(pl.BlockSpec((tm, d), lambda i: (i, 0)), pl.BlockSpec((tm, d), lambda i: (i, 0))),
        out_shape=(jax.ShapeDtypeStruct((m, d), F32), jax.ShapeDtypeStruct((m, d), BF16)),
        compiler_params=_params(("parallel",), vmem),
        name="out_proj_norm_res",
    )(h, a1, a2, w, gain, gain_next)


def _ffn_kernel(xn_ref, wg_ref, wv_ref, cwg_ref, cwv_ref, cbg_ref, cbv_ref,
                wd_ref, g2_ref, o_ref, tail_ref, *, tiles_per_seq):
    i, j = pl.program_id(0), pl.program_id(1)
    tm = xn_ref.shape[0]

    @pl.when(j == 0)
    def _():
        o_ref[...] = jnp.zeros_like(o_ref)

    @pl.when((i == 0) & (j == 0))
    def _():
        tail_ref[...] = jnp.zeros_like(tail_ref)

    xn = xn_ref[...]
    first = (i % tiles_per_seq) == 0

    def conv(w_ref, cw_ref, cb_ref, slot):
        u = _dot(xn, w_ref[...])
        above = jnp.where(first, 0.0, tail_ref[slot, j])
        tail_ref[slot, j] = u[tm - CARRY:tm]
        u = jnp.concatenate([above, u], axis=0)
        cw = cw_ref[...]
        y = cb_ref[...] + cw[2:3] * u + cw[1:2] * pltpu.roll(u, 1, 0) + cw[0:1] * pltpu.roll(u, 2, 0)
        return y[CARRY:CARRY + tm]

    gate = conv(wg_ref, cwg_ref, cbg_ref, 0)
    val = conv(wv_ref, cwv_ref, cbv_ref, 1)
    mid = (jax.nn.gelu(gate, approximate=True) * val).astype(BF16)
    o_ref[...] += _dot(mid, wd_ref[...])

    @pl.when(j == pl.num_programs(1) - 1)
    def _():
        o_ref[...] = _rms(o_ref[...], g2_ref[...])


def conv_gated_mlp(xn, seq_len, w_up, conv_w, conv_b, w_down, g2, layer, *, tm=1024, tf=512):
    m, d = xn.shape
    f = w_down.shape[1]
    tm, tf = min(tm, seq_len), min(tf, f)
    nf = f // tf
    assert seq_len % tm == 0 and f % tf == 0 and tm % CARRY == 0 and CARRY >= CONV_WIDTH - 1
    vmem = (2 * tm * d * 4 + 2 * tm * d * 2 + 3 * 2 * d * tf * 2
            + 4 * (tm + CARRY) * tf * 4 + 2 * nf * CARRY * tf * 4) // MIB + 8
    gain = pl.BlockSpec((None, 1, d), lambda i, j: (layer, 0, 0))
    return pl.pallas_call(
        functools.partial(_ffn_kernel, tiles_per_seq=seq_len // tm),
        grid=(m // tm, nf),
        in_specs=[
            pl.BlockSpec((tm, d), lambda i, j: (i, 0)),
            pl.BlockSpec((None, d, tf), lambda i, j: (layer, 0, j)),
            pl.BlockSpec((None, d, tf), lambda i, j: (layer, 0, nf + j)),
            pl.BlockSpec((None, CONV_WIDTH, tf), lambda i, j: (layer, 0, j)),
            pl.BlockSpec((None, CONV_WIDTH, tf), lambda i, j: (layer, 0, nf + j)),
            pl.BlockSpec((None, 1, tf), lambda i, j: (layer, 0, j)),
            pl.BlockSpec((None, 1, tf), lambda i, j: (layer, 0, nf + j)),
            pl.BlockSpec((None, tf, d), lambda i, j: (layer, j, 0)),
            gain,
        ],
        out_specs=pl.BlockSpec((tm, d), lambda i, j: (i, 0)),
        out_shape=jax.ShapeDtypeStruct((m, d), F32),
        scratch_shapes=[pltpu.VMEM((2, nf, CARRY, tf), F32)],
        compiler_params=_params(("arbitrary", "arbitrary"), vmem),
        name="conv_gated_mlp",
    )(xn, w_up, w_up, conv_w, conv_w, conv_b, conv_b, w_down, g2)


def _ple_kernel(h_ref, y_ref, p_ref, wpe_ref, wpg_ref, o_ref):
    h = h_ref[...] + y_ref[...]
    emb = _dot(p_ref[...].astype(BF16), wpe_ref[...])
    gate = jax.nn.sigmoid(_dot(h.astype(BF16), wpg_ref[...]))
    o_ref[...] = h + emb * gate


def per_layer_embedding(h, y, p, w_pe, w_pg, layer, *, tm=512):
    m, d = h.shape
    pd = p.shape[-1]
    tm = min(tm, m)
    vmem = (6 * tm * d * 4 + 2 * tm * pd * 4 + 2 * (pd + d) * d * 2 + 3 * tm * d * 4) // MIB + 6
    return pl.pallas_call(
        _ple_kernel,
        grid=(m // tm,),
        in_specs=[
            pl.BlockSpec((tm, d), lambda i: (i, 0)),
            pl.BlockSpec((tm, d), lambda i: (i, 0)),
            pl.BlockSpec((None, tm, pd), lambda i: (layer, i, 0)),
            pl.BlockSpec((None, pd, d), lambda i: (layer, 0, 0)),
            pl.BlockSpec((None, d, d), lambda i: (layer, 0, 0)),
        ],
        out_specs=pl.BlockSpec((tm, d), lambda i: (i, 0)),
        out_shape=jax.ShapeDtypeStruct((m, d), F32),
        compiler_params=_params(("parallel",), vmem),
        name="per_layer_embedding",
    )(h, y, p, w_pe, w_pg)


def kernel(x, p, ln_mix_pre, w_in, lb_logits, hgrn_norm, w_out, ln_mix_post, ln_ffn_pre, w_up,
           conv_w, conv_b, w_down, ln_ffn_post, w_pe, w_pg):
    bsz, s, d = x.shape
    depth = w_in.shape[0]
    m = bsz * s
    row = lambda a: a.astype(F32).reshape(depth, 1, a.shape[-1])
    w_in_b, w_out_b, w_up_b, w_down_b = (w.astype(BF16) for w in (w_in, w_out, w_up, w_down))
    w_pe_b, w_pg_b = w_pe.astype(BF16), w_pg.astype(BF16)
    g_mix_pre, g_mix_post, g_ffn_pre, g_ffn_post = map(row, (ln_mix_pre, ln_mix_post, ln_ffn_pre, ln_ffn_post))
    hgrn_w, conv_bias = row(hgrn_norm), row(conv_b)
    lb_all = lower_bounds(lb_logits).reshape(depth, 1, GROUP_WIDTH)
    p2 = p.reshape(depth, m, p.shape[-1])

    h = x.reshape(m, d)
    for l in range(depth):
        n_rec = 4 * GROUP_WIDTH
        rec, xn = norm_matmul(h, g_mix_pre, w_in_b, l, n_rec)
        rec = rec.reshape(bsz, s, n_rec)
        planes = matmul_planes(xn.reshape(bsz, s, d), w_in_b, l, n_rec, 3)
        o_rec = hgrn2(rec, lb_all, hgrn_w, l)
        o_att = dilated_attention(planes)
        h, xn = out_proj(h, o_rec.reshape(m, GROUP_WIDTH), o_att.reshape(m, GROUP_WIDTH), w_out_b,
                         g_mix_post, g_ffn_pre, l)
        y = conv_gated_mlp(xn, s, w_up_b, conv_w, conv_bias, w_down_b, g_ffn_post, l)
        h = per_layer_embedding(h, y, p2, w_pe_b, w_pg_b, l)
    return h.reshape(bsz, s, d)
```
